```python
import math
import jax, jax.numpy as jnp
from jax import lax
import numpy as np

D_MODEL = 1024
BATCH = 4
SEQ = 4096
DEPTH = 2
DEC_BATCH = 32
DEC_SEQ = 1
PAST_LEN = 16384
PAGE_SIZE = 128

N_A_LAYERS = DEPTH // 2
N_B_LAYERS = DEPTH - N_A_LAYERS
SSM_EXPAND = 2
SSM_D_INNER = SSM_EXPAND * D_MODEL
SSM_HEAD_DIM = 64
SSM_HEADS = SSM_D_INNER // SSM_HEAD_DIM
SSM_GROUPS = 4
SSM_HEADS_PER_GROUP = SSM_HEADS // SSM_GROUPS
SSM_D_STATE = 128
SSM_CONV_W = 4
SSM_CONV_DIM = SSM_D_INNER + 2 * SSM_GROUPS * SSM_D_STATE
SSM_IN_DIM = SSM_D_INNER + SSM_CONV_DIM + SSM_HEADS
SSD_CHUNK = 128
ATTN_HEADS = 8
ATTN_HEAD_DIM = D_MODEL // ATTN_HEADS // 2
ATTN_V_DIM = 2 * ATTN_HEAD_DIM
ATTN_WIDTH = ATTN_HEADS * ATTN_V_DIM
ATTN_SCALE = ATTN_HEAD_DIM ** -0.5
Q_BLOCK = 128
D_FF = 2816
FFN_CONV_W = 3
EPS = 1e-5

kernel_name = "yoco_mamba2_diffattn_convffn_step"

F32 = jnp.float32


def rmsnorm(x, w):
    xf = x.astype(F32)
    y = xf * lax.rsqrt(jnp.mean(xf * xf, axis=-1, keepdims=True) + EPS) * w.astype(F32)
    return y.astype(x.dtype)


def causal_dwconv(x, buf, w, b):
    width = w.shape[0]
    L = x.shape[1]
    xp = jnp.concatenate([buf.astype(x.dtype), x], axis=1)
    out = xp[:, 0:L] * w[0]
    for k in range(1, width):
        out = out + xp[:, k:k + L] * w[k]
    return out + b, xp[:, L:]


def segsum(a):
    Q = a.shape[-1]
    ax = jnp.broadcast_to(a[..., :, None], a.shape + (Q,))
    ax = jnp.where(jnp.tril(jnp.ones((Q, Q), bool), -1), ax, 0.0)
    cs = jnp.cumsum(ax, axis=-2)
    return jnp.where(jnp.tril(jnp.ones((Q, Q), bool)), cs, -jnp.inf)


def ssd_scan(x, dt, A, Bm, Cm, h0):
    b, L, G, R, P = x.shape
    N = Bm.shape[-1]
    Q = SSD_CHUNK if L % SSD_CHUNK == 0 else L
    nc = L // Q
    xdt = (x.astype(F32) * dt[..., None]).reshape(b, nc, Q, G, R, P)
    dA = jnp.moveaxis((dt * A).reshape(b, nc, Q, G, R), 2, -1)
    a_cum = jnp.cumsum(dA, axis=-1)
    Lmat = jnp.exp(segsum(dA))
    Bc = Bm.astype(F32).reshape(b, nc, Q, G, N)
    Cc = Cm.astype(F32).reshape(b, nc, Q, G, N)
    cb = jnp.einsum('bcqgn,bcsgn->bcgqs', Cc, Bc)
    y_diag = jnp.einsum('bcgqs,bcgrqs,bcsgrp->bcqgrp', cb, Lmat, xdt)
    decay = jnp.exp(a_cum[..., -1:] - a_cum)
    states = jnp.einsum('bcsgn,bcgrs,bcsgrp->bcgrpn', Bc, decay, xdt)
    chunk_decay = jnp.exp(a_cum[..., -1])

    def step(h, inp):
        st, dec = inp
        return h * dec[..., None, None] + st, h

    h_final, h_prev = lax.scan(step, h0.astype(F32),
                               (jnp.moveaxis(states, 1, 0), jnp.moveaxis(chunk_decay, 1, 0)))
    h_prev = jnp.moveaxis(h_prev, 0, 1)
    y_off = jnp.einsum('bcqgn,bcgrpn,bcgrq->bcqgrp', Cc, h_prev, jnp.exp(a_cum))
    return (y_diag + y_off).reshape(b, L, G, R, P), h_final


def mamba2_mixer(u, conv_buf, h0, w_in, conv_w, conv_b, dt_bias, a_log, d_skip, norm_w, w_out):
    b, L, _ = u.shape
    zxbcdt = u @ w_in
    z = zxbcdt[..., :SSM_D_INNER]
    xbc = zxbcdt[..., SSM_D_INNER:SSM_D_INNER + SSM_CONV_DIM]
    dt_raw = zxbcdt[..., SSM_D_INNER + SSM_CONV_DIM:]
    xbc, new_buf = causal_dwconv(xbc, conv_buf, conv_w, conv_b)
    xbc = jax.nn.silu(xbc)
    gn = SSM_GROUPS * SSM_D_STATE
    xs = xbc[..., :SSM_D_INNER].reshape(b, L, SSM_GROUPS, SSM_HEADS_PER_GROUP, SSM_HEAD_DIM)
    Bm = xbc[..., SSM_D_INNER:SSM_D_INNER + gn].reshape(b, L, SSM_GROUPS, SSM_D_STATE)
    Cm = xbc[..., SSM_D_INNER + gn:].reshape(b, L, SSM_GROUPS, SSM_D_STATE)
    dt = jax.nn.softplus(dt_raw.astype(F32) + dt_bias.astype(F32)).reshape(b, L, SSM_GROUPS, SSM_HEADS_PER_GROUP)
    A = -jnp.exp(a_log.astype(F32)).reshape(SSM_GROUPS, SSM_HEADS_PER_GROUP)
    h0g = h0.reshape(b, SSM_GROUPS, SSM_HEADS_PER_GROUP, SSM_HEAD_DIM, SSM_D_STATE)
    y, h_final = ssd_scan(xs, dt, A, Bm, Cm, h0g)
    y = y + d_skip.astype(F32).reshape(SSM_GROUPS, SSM_HEADS_PER_GROUP, 1) * xs.astype(F32)
    g = (y.reshape(b, L, SSM_D_INNER) * jax.nn.silu(z.astype(F32))).reshape(b, L, SSM_GROUPS, -1)
    g = g * lax.rsqrt(jnp.mean(g * g, axis=-1, keepdims=True) + EPS)
    g = (g.reshape(b, L, SSM_D_INNER) * norm_w.astype(F32)).astype(u.dtype)
    h_final = h_final.reshape(b, SSM_HEADS, SSM_HEAD_DIM, SSM_D_STATE).astype(u.dtype)
    return g @ w_out, new_buf, h_final


def conv_ffn(u, buf, w_up, conv_w, conv_b, w_down):
    gu = u @ w_up
    g, up = gu[..., :D_FF], gu[..., D_FF:]
    g, new_buf = causal_dwconv(g, buf, conv_w, conv_b)
    return (jax.nn.silu(g) * up) @ w_down, new_buf


def shared_kv(x, norm_w, w_k, w_v):
    kv_in = rmsnorm(x, norm_w)
    b, L, _ = x.shape
    k = (kv_in @ w_k).reshape(b, L, ATTN_HEADS, ATTN_V_DIM)
    v = (kv_in @ w_v).reshape(b, L, ATTN_HEADS, ATTN_V_DIM)
    return k, v


def diff_lambda(lq1, lk1, lq2, lk2, lambda_init):
    return (jnp.exp(jnp.sum(lq1.astype(F32) * lk1.astype(F32)))
            - jnp.exp(jnp.sum(lq2.astype(F32) * lk2.astype(F32))) + lambda_init)


def diff_out(o, subln_w, w_o, lambda_init, dtype):
    b, L = o.shape[:2]
    o = o * lax.rsqrt(jnp.mean(o * o, axis=-1, keepdims=True) + EPS) * subln_w.astype(F32)
    o = (o * (1.0 - lambda_init)).astype(dtype).reshape(b, L, ATTN_WIDTH)
    return o @ w_o


def diff_attn_prompt(h, k, v, w_q, lam, subln_w, w_o, lambda_init):
    b, L, _ = h.shape
    nb = L // Q_BLOCK
    q = (h @ w_q).astype(F32).reshape(b, nb, Q_BLOCK, ATTN_HEADS, 2, ATTN_HEAD_DIM) * ATTN_SCALE
    q = jnp.moveaxis(q, 1, 0)
    kf = k.astype(F32).reshape(b, L, ATTN_HEADS, 2, ATTN_HEAD_DIM)
    vf = v.astype(F32)
    kpos = jnp.arange(L)

    def block(args):
        qblk, i = args
        qpos = i * Q_BLOCK + jnp.arange(Q_BLOCK)
        s = jnp.einsum('bqhcd,bshcd->bchqs', qblk, kf)
        s = jnp.where(qpos[:, None] >= kpos[None, :], s, -jnp.inf)
        p = jax.nn.softmax(s, axis=-1)
        a = p[:, 0] - lam * p[:, 1]
        return jnp.einsum('bhqs,bshe->bqhe', a, vf)

    o = lax.map(block, (q, jnp.arange(nb)))
    o = jnp.moveaxis(o, 0, 1).reshape(b, L, ATTN_HEADS, ATTN_V_DIM)
    return diff_out(o, subln_w, w_o, lambda_init, h.dtype)


def diff_attn_sample(h, k_new, v_new, cache_k, cache_v, page_table, w_q, lam, subln_w, w_o, lambda_init):
    b, T, _ = h.shape
    q = (h @ w_q).astype(F32).reshape(b, T, ATTN_HEADS, 2, ATTN_HEAD_DIM) * ATTN_SCALE

    def update(carry, k, v, mask):
        m, l, acc = carry
        S = k.shape[1]
        s = jnp.einsum('bthcd,bshcd->bchts', q, k.astype(F32).reshape(b, S, ATTN_HEADS, 2, ATTN_HEAD_DIM))
        if mask is not None:
            s = jnp.where(mask, s, -jnp.inf)
        m_new = jnp.maximum(m, jnp.max(s, axis=-1))
        corr = jnp.exp(m - m_new)
        p = jnp.exp(s - m_new[..., None])
        l = l * corr + jnp.sum(p, axis=-1)
        acc = acc * corr[..., None] + jnp.einsum('bchts,bshe->bchte', p, v.astype(F32))
        return (m_new, l, acc)

    init = (jnp.full((b, 2, ATTN_HEADS, T), -jnp.inf, F32),
            jnp.zeros((b, 2, ATTN_HEADS, T), F32),
            jnp.zeros((b, 2, ATTN_HEADS, T, ATTN_V_DIM), F32))

    def page_step(carry, pages):
        return update(carry, cache_k[pages], cache_v[pages], None), None

    carry, _ = lax.scan(page_step, init, page_table.T)
    carry = update(carry, k_new, v_new, jnp.tril(jnp.ones((T, T), bool)))
    _, l, acc = carry
    o = acc / l[..., None]
    o = o[:, 0] - lam * o[:, 1]
    o = jnp.moveaxis(o, 2, 1)
    return diff_out(o, subln_w, w_o, lambda_init, h.dtype)


def setup_inputs(seed: int = 0) -> dict:
    key = jax.random.key(seed)
    ks = iter(jax.random.split(key, 48))

    def nrm(shape, scale):
        return jax.random.normal(next(ks), shape, F32) * scale

    n_pages = PAST_LEN // PAGE_SIZE
    n_used = DEC_BATCH * n_pages
    n_phys = n_used + (n_used + 3) // 4
    page_table = jax.random.permutation(next(ks), n_phys)[:n_used].reshape(DEC_BATCH, n_pages).astype(jnp.int32)
    dt0 = jnp.exp(jax.random.uniform(next(ks), (N_A_LAYERS, SSM_HEADS), F32,
                                     minval=math.log(1e-3), maxval=math.log(1e-1)))
    ssm_dt_bias = dt0 + jnp.log(-jnp.expm1(-dt0))
    ssm_a_log = jnp.log(jax.random.uniform(next(ks), (N_A_LAYERS, SSM_HEADS), F32, minval=1.0, maxval=16.0))
    return {
        "x_prompt": nrm((BATCH, SEQ, D_MODEL), 1.0),
        "x_sample": nrm((DEC_BATCH, DEC_SEQ, D_MODEL), 1.0),
        "state_ssm": nrm((N_A_LAYERS, DEC_BATCH, SSM_HEADS, SSM_HEAD_DIM, SSM_D_STATE), 0.5),
        "state_conv_ssm": nrm((N_A_LAYERS, DEC_BATCH, SSM_CONV_W - 1, SSM_CONV_DIM), 1.0),
        "state_conv_ffn": nrm((DEPTH, DEC_BATCH, FFN_CONV_W - 1, D_FF), 1.0),
        "cache_k": nrm((n_phys, PAGE_SIZE, ATTN_HEADS, ATTN_V_DIM), 1.0),
        "cache_v": nrm((n_phys, PAGE_SIZE, ATTN_HEADS, ATTN_V_DIM), 1.0),
        "page_table": page_table,
        "norm_mix": 1.0 + nrm((DEPTH, D_MODEL), 0.02),
        "norm_ffn": 1.0 + nrm((DEPTH, D_MODEL), 0.02),
        "ssm_in_proj": nrm((N_A_LAYERS, D_MODEL, SSM_IN_DIM), D_MODEL ** -0.5),
        "ssm_conv_w": nrm((N_A_LAYERS, SSM_CONV_W, SSM_CONV_DIM), SSM_CONV_W ** -0.5),
        "ssm_conv_b": nrm((N_A_LAYERS, SSM_CONV_DIM), 0.02),
        "ssm_dt_bias": ssm_dt_bias,
        "ssm_a_log": ssm_a_log,
        "ssm_d": 1.0 + nrm((N_A_LAYERS, SSM_HEADS), 0.1),
        "ssm_norm": 1.0 + nrm((N_A_LAYERS, SSM_D_INNER), 0.02),
        "ssm_out_proj": nrm((N_A_LAYERS, SSM_D_INNER, D_MODEL), SSM_D_INNER ** -0.5),
        "ffn_up": nrm((DEPTH, D_MODEL, 2 * D_FF), D_MODEL ** -0.5),
        "ffn_conv_w": nrm((DEPTH, FFN_CONV_W, D_FF), FFN_CONV_W ** -0.5),
        "ffn_conv_b": nrm((DEPTH, D_FF), 0.02),
        "ffn_down": nrm((DEPTH, D_FF, D_MODEL), D_FF ** -0.5),
        "norm_kv": 1.0 + nrm((D_MODEL,), 0.02),
        "w_k": nrm((D_MODEL, ATTN_WIDTH), D_MODEL ** -0.5),
        "w_v": nrm((D_MODEL, ATTN_WIDTH), D_MODEL ** -0.5),
        "w_q": nrm((N_B_LAYERS, D_MODEL, ATTN_WIDTH), D_MODEL ** -0.5),
        "lambda_q1": nrm((N_B_LAYERS, ATTN_HEAD_DIM), 0.1),
        "lambda_k1": nrm((N_B_LAYERS, ATTN_HEAD_DIM), 0.1),
        "lambda_q2": nrm((N_B_LAYERS, ATTN_HEAD_DIM), 0.1),
        "lambda_k2": nrm((N_B_LAYERS, ATTN_HEAD_DIM), 0.1),
        "subln": 1.0 + nrm((N_B_LAYERS, ATTN_V_DIM), 0.02),
        "w_o": nrm((N_B_LAYERS, ATTN_WIDTH, D_MODEL), ATTN_WIDTH ** -0.5),
        "norm_final": 1.0 + nrm((D_MODEL,), 0.02),
    }


def reference(x_prompt, x_sample, state_ssm, state_conv_ssm, state_conv_ffn, cache_k, cache_v, page_table,
              norm_mix, norm_ffn, ssm_in_proj, ssm_conv_w, ssm_conv_b, ssm_dt_bias, ssm_a_log, ssm_d,
              ssm_norm, ssm_out_proj, ffn_up, ffn_conv_w, ffn_conv_b, ffn_down, norm_kv, w_k, w_v,
              w_q, lambda_q1, lambda_k1, lambda_q2, lambda_k2, subln, w_o, norm_final):
    xp, xs = x_prompt, x_sample
    bp, bd = xp.shape[0], xs.shape[0]
    ssm_p, ssm_s, cssm_p, cssm_s, cffn_p, cffn_s = [], [], [], [], [], []
    kp = vp = ks = vs = None
    for i in range(DEPTH):
        if i < N_A_LAYERS:
            a = i
            params = (ssm_in_proj[a], ssm_conv_w[a], ssm_conv_b[a], ssm_dt_bias[a], ssm_a_log[a],
                      ssm_d[a], ssm_norm[a], ssm_out_proj[a])
            buf0 = jnp.zeros((bp, SSM_CONV_W - 1, SSM_CONV_DIM), xp.dtype)
            h00 = jnp.zeros((bp, SSM_HEADS, SSM_HEAD_DIM, SSM_D_STATE), F32)
            op, bufp, hp = mamba2_mixer(rmsnorm(xp, norm_mix[i]), buf0, h00, *params)
            os_, bufs, hs = mamba2_mixer(rmsnorm(xs, norm_mix[i]), state_conv_ssm[a], state_ssm[a], *params)
            xp, xs = xp + op, xs + os_
            ssm_p.append(hp); ssm_s.append(hs); cssm_p.append(bufp); cssm_s.append(bufs)
        else:
            j = i - N_A_LAYERS
            lambda_init = 0.8 - 0.6 * math.exp(-0.3 * i)
            lam = diff_lambda(lambda_q1[j], lambda_k1[j], lambda_q2[j], lambda_k2[j], lambda_init)
            op = diff_attn_prompt(rmsnorm(xp, norm_mix[i]), kp, vp, w_q[j], lam, subln[j], w_o[j], lambda_init)
            os_ = diff_attn_sample(rmsnorm(xs, norm_mix[i]), ks, vs, cache_k, cache_v, page_table,
                                   w_q[j], lam, subln[j], w_o[j], lambda_init)
            xp, xs = xp + op, xs + os_
        fbuf0 = jnp.zeros((bp, FFN_CONV_W - 1, D_FF), xp.dtype)
        fp, fbufp = conv_ffn(rmsnorm(xp, norm_ffn[i]), fbuf0, ffn_up[i], ffn_conv_w[i], ffn_conv_b[i], ffn_down[i])
        fs, fbufs = conv_ffn(rmsnorm(xs, norm_ffn[i]), state_conv_ffn[i], ffn_up[i], ffn_conv_w[i],
                             ffn_conv_b[i], ffn_down[i])
        xp, xs = xp + fp, xs + fs
        cffn_p.append(fbufp); cffn_s.append(fbufs)
        if i == N_A_LAYERS - 1:
            kp, vp = shared_kv(xp, norm_kv, w_k, w_v)
            ks, vs = shared_kv(xs, norm_kv, w_k, w_v)
    y_prompt = rmsnorm(xp, norm_final)
    y_sample = rmsnorm(xs, norm_final)
    return (y_prompt, y_sample,
            jnp.stack(ssm_p), jnp.stack(ssm_s), jnp.stack(cssm_p), jnp.stack(cssm_s),
            jnp.stack(cffn_p), jnp.stack(cffn_s), kp, vp, ks, vs)
```

```python
import functools
import math

import jax
import jax.numpy as jnp
from jax import lax
from jax.experimental import pallas as pl
from jax.experimental.pallas import tpu as pltpu

F32 = jnp.float32
BF16 = jnp.bfloat16
EPS = 1e-5
NEG = -1e30

SSM_HEAD_DIM = 64
SSM_GROUPS = 4
SSM_D_STATE = 128
SSM_CONV_W = 4
SSD_CHUNK = 128
ATTN_HEADS = 8
FFN_CONV_W = 3
PAGES_PER_STEP = 8

VMEM_LIMIT_BYTES = 56 * 1024 * 1024


def _cparams(*sem):
    return pltpu.CompilerParams(dimension_semantics=sem, vmem_limit_bytes=VMEM_LIMIT_BYTES)


def _silu(x):
    return x / (1.0 + jnp.exp(-x))


def _rms_unit(x):
    return x * lax.rsqrt(jnp.mean(x * x, axis=-1, keepdims=True) + EPS)


def _split3(a):
    hi = a.astype(BF16)
    r = a - hi.astype(F32)
    mid = r.astype(BF16)
    lo = (r - mid.astype(F32)).astype(BF16)
    return hi, mid, lo


def _dot(a, b):
    return jnp.dot(a, b, preferred_element_type=F32)


def _dot_nt(a, b):
    return lax.dot_general(a, b, (((1,), (1,)), ((), ())), preferred_element_type=F32)


def _dot_exact_rhs(a, b_bf16):
    hi, mid, lo = _split3(a)
    return _dot(hi, b_bf16) + _dot(mid, b_bf16) + _dot(lo, b_bf16)


def _dot_exact_lhs(a_bf16, b):
    hi, mid, lo = _split3(b)
    return _dot(a_bf16, hi) + _dot(a_bf16, mid) + _dot(a_bf16, lo)


def _row_tile(m, pref):
    t = min(m, pref)
    assert m % t == 0, (m, t)
    return t


def _norm_mm_kernel(n_w, gain_idx, scales, out_dtypes, x_ref, g_ref, *rest):
    w_refs = rest[:n_w]
    o_refs = rest[n_w:-1]
    xn_ref = rest[-1]

    @pl.when(pl.program_id(1) == 0)
    def _():
        xh = _rms_unit(x_ref[...])
        for i in range(n_w):
            gi = gain_idx[i]
            xn_ref[i] = (xh * g_ref[gi:gi + 1, :]).astype(BF16)

    k = 0
    for i in range(n_w):
        r = _dot(xn_ref[i], w_refs[i][...])
        if scales[i] != 1.0:
            r = r * scales[i]
        for dt in out_dtypes[i]:
            o_refs[k][...] = r.astype(dt)
            k += 1


def norm_matmul(x, gains, gain_idx, weights, out_dtypes, scales=None, tm=512, tn=None, name="norm_mm"):
    m, kdim = x.shape
    n = weights[0].shape[1]
    n_w = len(weights)
    scales = tuple(scales) if scales is not None else (1.0,) * n_w
    tm = _row_tile(m, tm)
    tn = n if tn is None else tn
    assert n % tn == 0
    outs, specs = [], []
    for dts in out_dtypes:
        for dt in dts:
            outs.append(jax.ShapeDtypeStruct((m, n), dt))
            specs.append(pl.BlockSpec((tm, tn), lambda i, j: (i, j)))
    kern = functools.partial(_norm_mm_kernel, n_w, tuple(gain_idx), scales,
                             tuple(tuple(d) for d in out_dtypes))
    return pl.pallas_call(
        kern,
        grid=(m // tm, n // tn),
        in_specs=[pl.BlockSpec((tm, kdim), lambda i, j: (i, 0)),
                  pl.BlockSpec(gains.shape, lambda i, j: (0, 0))]
        + [pl.BlockSpec((kdim, tn), lambda i, j: (0, j)) for _ in weights],
        out_specs=specs,
        out_shape=outs,
        scratch_shapes=[pltpu.VMEM((n_w, tm, kdim), BF16)],
        compiler_params=_cparams("parallel", "arbitrary"),
        name=name,
    )(x, gains, *weights)


def _mm_res_kernel(final_norm, a_ref, w_ref, r_ref, *rest):
    o_ref = rest[-1]
    y = r_ref[...] + _dot(a_ref[...], w_ref[...])
    if final_norm:
        y = _rms_unit(y) * rest[0][...]
    o_ref[...] = y


def matmul_residual(a, w, res, final_gain=None, tm=512, name="mm_res"):
    m, kdim = a.shape
    n = w.shape[1]
    tm = _row_tile(m, tm)
    ins = [a, w, res]
    specs = [pl.BlockSpec((tm, kdim), lambda i: (i, 0)),
             pl.BlockSpec((kdim, n), lambda i: (0, 0)),
             pl.BlockSpec((tm, n), lambda i: (i, 0))]
    if final_gain is not None:
        ins.append(final_gain)
        specs.append(pl.BlockSpec((1, n), lambda i: (0, 0)))
    return pl.pallas_call(
        functools.partial(_mm_res_kernel, final_gain is not None),
        grid=(m // tm,),
        in_specs=specs,
        out_specs=pl.BlockSpec((tm, n), lambda i: (i, 0)),
        out_shape=jax.ShapeDtypeStruct((m, n), F32),
        compiler_params=_cparams("parallel"),
        name=name,
    )(*ins)


HALO = 16


def _ffn_up_kernel(tm, seq, x_ref, halo_ref, g_ref, wg_ref, wu_ref, cw_ref, cb_ref,
                   h_ref, tail_ref, xn_ref):
    i = pl.program_id(0)

    @pl.when(pl.program_id(1) == 0)
    def _():
        gain = g_ref[...]
        xn_ref[HALO:, :] = (_rms_unit(x_ref[...]) * gain).astype(BF16)
        keep = jnp.where((i * tm) % seq == 0, 0.0, 1.0)
        xn_ref[0:HALO, :] = (_rms_unit(halo_ref[...]) * gain * keep).astype(BF16)

    g = _dot(xn_ref[...], wg_ref[...])
    u = _dot(xn_ref[HALO:, :], wu_ref[...])
    c = (cw_ref[2:3, :] * g + cw_ref[1:2, :] * pltpu.roll(g, 1, 0)
         + cw_ref[0:1, :] * pltpu.roll(g, 2, 0) + cb_ref[...])
    c = c[HALO:, :]
    h_ref[...] = (_silu(c) * u).astype(BF16)
    tail_ref[0] = g[tm + HALO - 8:, :]


def ffn_up_prompt(x, gain, w_up, cw, cb, seq, tm=512, tn=None, name="ffn_up"):
    m, d = x.shape
    f = cw.shape[1]
    tm = _row_tile(seq, tm)
    tn = f if tn is None else tn
    assert f % tn == 0 and tm % HALO == 0
    nj = f // tn
    hb = tm // HALO
    h, tail = pl.pallas_call(
        functools.partial(_ffn_up_kernel, tm, seq),
        grid=(m // tm, nj),
        in_specs=[pl.BlockSpec((tm, d), lambda i, j: (i, 0)),
                  pl.BlockSpec((HALO, d), lambda i, j: (jnp.maximum(i * hb - 1, 0), 0)),
                  pl.BlockSpec((1, d), lambda i, j: (0, 0)),
                  pl.BlockSpec((d, tn), lambda i, j: (0, j)),
                  pl.BlockSpec((d, tn), lambda i, j: (0, nj + j)),
                  pl.BlockSpec((FFN_CONV_W, tn), lambda i, j: (0, j)),
                  pl.BlockSpec((1, tn), lambda i, j: (0, j))],
        out_specs=[pl.BlockSpec((tm, tn), lambda i, j: (i, j)),
                   pl.BlockSpec((1, 8, tn), lambda i, j: (i, 0, j))],
        out_shape=[jax.ShapeDtypeStruct((m, f), BF16),
                   jax.ShapeDtypeStruct((m // tm, 8, f), F32)],
        scratch_shapes=[pltpu.VMEM((HALO + tm, d), BF16)],
        compiler_params=_cparams("parallel", "arbitrary"),
        name=name,
    )(x, x, gain, w_up, w_up, cw, cb)
    return h, tail


def _ssd_kernel(d_inner, n_heads, zx_ref, cw_ref, cb_ref, dtb_ref, alog_ref, dskip_ref, nw_ref,
                expand_ref, g_ref, tail_ref, state_ref, xpad_ref, ht_ref, y_ref):
    q = SSD_CHUNK
    ns = SSM_D_STATE
    gn = SSM_GROUPS * ns
    conv_dim = d_inner + 2 * gn
    gw = d_inner // SSM_GROUPS
    c = pl.program_id(1)
    last = pl.num_programs(1) - 1

    @pl.when(c == 0)
    def _():
        xpad_ref[0:8, :] = jnp.zeros((8, conv_dim), F32)
        ht_ref[...] = jnp.zeros(ht_ref.shape, F32)

    xpad_ref[8:, :] = zx_ref[:, d_inner:d_inner + conv_dim]
    conv = cb_ref[...] + cw_ref[SSM_CONV_W - 1:SSM_CONV_W, :] * xpad_ref[8:, :]
    for k in range(1, SSM_CONV_W):
        conv = conv + cw_ref[SSM_CONV_W - 1 - k:SSM_CONV_W - k, :] * xpad_ref[pl.ds(8 - k, q), :]
    new_tail = xpad_ref[q:q + 8, :]
    xpad_ref[0:8, :] = new_tail
    xbc = _silu(conv)
    xs = xbc[:, :d_inner]
    bm = xbc[:, d_inner:d_inner + gn]
    cm = xbc[:, d_inner + gn:]

    dt_raw = zx_ref[:, d_inner + conv_dim:d_inner + conv_dim + 128] + dtb_ref[...]
    dt = jnp.maximum(dt_raw, 0.0) + jnp.log1p(jnp.exp(-jnp.abs(dt_raw)))
    da = dt * (-jnp.exp(alog_ref[...]))
    row = lax.broadcasted_iota(jnp.int32, (q, q), 0)
    col = lax.broadcasted_iota(jnp.int32, (q, q), 1)
    tri = row >= col
    a_cum = _dot_exact_lhs(jnp.where(tri, 1.0, 0.0).astype(BF16), da)
    a_cum_t = jnp.transpose(a_cum)

    expand = expand_ref[...]
    dt_hp = _dot_exact_rhs(dt, expand)
    a_hp = _dot_exact_rhs(a_cum, expand)
    xdt = xs * dt_hp
    a_last = a_hp[q - 1:q, :]
    xdt_b = xdt.astype(BF16)
    dec_x = (jnp.exp(a_last - a_hp) * xdt).astype(BF16)
    off_scale = jnp.exp(a_hp)
    chunk_decay = jnp.exp(a_last)

    lane = lax.broadcasted_iota(jnp.int32, (q, 2 * SSM_HEAD_DIM), 1)
    heads_per_group = n_heads // SSM_GROUPS
    for g in range(SSM_GROUPS):
        bg = bm[:, g * ns:(g + 1) * ns]
        cg = cm[:, g * ns:(g + 1) * ns].astype(BF16)
        cb_mat = _dot_nt(cg, bg.astype(BF16))
        cols = slice(g * gw, (g + 1) * gw)
        ht = ht_ref[g]
        y_off = _dot(cg, ht.astype(BF16)) * off_scale[:, cols]
        new_states = _dot(jnp.transpose(bg).astype(BF16), dec_x[:, cols])
        ht_ref[g] = ht * chunk_decay[:, cols] + new_states
        for pr in range(heads_per_group // 2):
            c0 = g * gw + pr * 2 * SSM_HEAD_DIM
            x2 = xdt_b[:, c0:c0 + 2 * SSM_HEAD_DIM]
            yd = None
            for half in range(2):
                hh = g * heads_per_group + pr * 2 + half
                seg = a_cum[:, hh:hh + 1] - a_cum_t[hh:hh + 1, :]
                lmat = jnp.where(tri, jnp.exp(jnp.minimum(seg, 0.0)), 0.0)
                mm = (cb_mat * lmat).astype(BF16)
                xh = jnp.where((lane >= SSM_HEAD_DIM) == (half == 1), x2, jnp.zeros_like(x2))
                t = _dot(mm, xh)
                yd = t if yd is None else yd + t
            y_ref[:, c0:c0 + 2 * SSM_HEAD_DIM] = yd + y_off[:, pr * 2 * SSM_HEAD_DIM:(pr + 1) * 2 * SSM_HEAD_DIM]

    y = y_ref[...] + dskip_ref[...] * xs
    gated = y * _silu(zx_ref[:, 0:d_inner])
    for g in range(SSM_GROUPS):
        cols = slice(g * gw, (g + 1) * gw)
        g_ref[:, cols] = (_rms_unit(gated[:, cols]) * nw_ref[:, cols]).astype(BF16)

    @pl.when(c == last)
    def _():
        tail_ref[0] = new_tail
        for g in range(SSM_GROUPS):
            st = jnp.transpose(ht_ref[g])
            state_ref[0, g * heads_per_group:(g + 1) * heads_per_group] = st.reshape(
                heads_per_group, SSM_HEAD_DIM, ns)


def ssd_prompt(zx, batch, seq, d_inner, cw, cb, dtb, alog, dskip_hp, nw, expand, name="ssd"):
    m, width = zx.shape
    n_heads = d_inner // SSM_HEAD_DIM
    conv_dim = cw.shape[1]
    nc = seq // SSD_CHUNK
    q = SSD_CHUNK
    full = lambda shape: pl.BlockSpec(shape, lambda b, c: (0,) * len(shape))
    return pl.pallas_call(
        functools.partial(_ssd_kernel, d_inner, n_heads),
        grid=(batch, nc),
        in_specs=[pl.BlockSpec((q, width), lambda b, c: (b * nc + c, 0)),
                  full(cw.shape), full(cb.shape), full(dtb.shape), full(alog.shape),
                  full(dskip_hp.shape), full(nw.shape), full(expand.shape)],
        out_specs=[pl.BlockSpec((q, d_inner), lambda b, c: (b * nc + c, 0)),
                   pl.BlockSpec((1, 8, conv_dim), lambda b, c: (b, 0, 0)),
                   pl.BlockSpec((1, n_heads, SSM_HEAD_DIM, SSM_D_STATE), lambda b, c: (b, 0, 0, 0))],
        out_shape=[jax.ShapeDtypeStruct((m, d_inner), BF16),
                   jax.ShapeDtypeStruct((batch, 8, conv_dim), F32),
                   jax.ShapeDtypeStruct((batch, n_heads, SSM_HEAD_DIM, SSM_D_STATE), F32)],
        scratch_shapes=[pltpu.VMEM((q + 8, conv_dim), F32),
                        pltpu.VMEM((SSM_GROUPS, SSM_D_STATE, d_inner // SSM_GROUPS), F32),
                        pltpu.VMEM((q, d_inner), F32)],
        compiler_params=_cparams("parallel", "arbitrary"),
        name=name,
    )(zx, cw, cb, dtb, alog, dskip_hp, nw, expand)


def _diff_lambda(lv, lambda_init):
    s1 = jnp.sum(lv[0:1, :] * lv[1:2, :], axis=-1, keepdims=True)
    s2 = jnp.sum(lv[2:3, :] * lv[3:4, :], axis=-1, keepdims=True)
    return jnp.exp(s1) - jnp.exp(s2) + lambda_init


def _attn_kernel(tq, lambda_init, q_ref, k_ref, v_ref, lv_ref, sub_ref, o_ref):
    qi = pl.program_id(2)
    qh = q_ref[0]
    hd = qh.shape[1] // 2
    lane = lax.broadcasted_iota(jnp.int32, qh.shape, 1)
    zero = jnp.zeros_like(qh)
    q2 = jnp.concatenate([jnp.where(lane < hd, qh, zero), jnp.where(lane >= hd, qh, zero)], axis=0)

    def block(kb, carry, masked):
        m, l, acc = carry
        start = pl.multiple_of(kb * tq, tq)
        k = k_ref[0, pl.ds(start, tq), :]
        v = v_ref[0, pl.ds(start, tq), :]
        s = _dot_nt(q2, k)
        if masked:
            r = lax.broadcasted_iota(jnp.int32, (tq, tq), 0)
            cidx = lax.broadcasted_iota(jnp.int32, (tq, tq), 1)
            ok = jnp.concatenate([r >= cidx, r >= cidx], axis=0)
            s = jnp.where(ok, s, -jnp.inf)
        m_new = jnp.maximum(m, jnp.max(s, axis=-1, keepdims=True))
        corr = jnp.exp(m - m_new)
        p = jnp.exp(s - m_new)
        l = l * corr + jnp.sum(p, axis=-1, keepdims=True)
        acc = acc * corr + _dot(p.astype(BF16), v)
        return m_new, l, acc

    init = (jnp.full((2 * tq, 1), -jnp.inf, F32), jnp.zeros((2 * tq, 1), F32),
            jnp.zeros((2 * tq, 2 * hd), F32))
    carry = lax.fori_loop(0, qi, lambda kb, cr: block(kb, cr, False), init)
    _, l, acc = block(qi, carry, True)
    o = acc / l
    lam = _diff_lambda(lv_ref[...], lambda_init)
    o = o[:tq] - lam * o[tq:]
    o = _rms_unit(o) * sub_ref[...]
    o_ref[0] = (o * (1.0 - lambda_init)).astype(BF16)


def diff_attention_prompt(q, k, v, lvec, subln, lambda_init, tq=512, name="diff_attn"):
    b, l, width = q.shape
    hw = width // ATTN_HEADS
    tq = _row_tile(l, tq)
    return pl.pallas_call(
        functools.partial(_attn_kernel, tq, lambda_init),
        grid=(b, ATTN_HEADS, l // tq),
        in_specs=[pl.BlockSpec((1, tq, hw), lambda bi, h, i: (bi, i, h)),
                  pl.BlockSpec((1, l, hw), lambda bi, h, i: (bi, 0, h)),
                  pl.BlockSpec((1, l, hw), lambda bi, h, i: (bi, 0, h)),
                  pl.BlockSpec(lvec.shape, lambda bi, h, i: (0, 0)),
                  pl.BlockSpec(subln.shape, lambda bi, h, i: (0, 0))],
        out_specs=pl.BlockSpec((1, tq, hw), lambda bi, h, i: (bi, i, h)),
        out_shape=jax.ShapeDtypeStruct((b, l, width), BF16),
        compiler_params=_cparams("parallel", "parallel", "arbitrary"),
        name=name,
    )(q, k, v, lvec, subln)


def _ssm_prep_kernel(d_inner, zx_ref, buf_ref, cw_ref, cb_ref, dtb_ref, alog_ref, expand_ref,
                     newbuf_ref, xs_ref, xdt_ref, b_ref, c_ref, decay_ref):
    gn = SSM_GROUPS * SSM_D_STATE
    conv_dim = d_inner + 2 * gn
    xnew = zx_ref[:, d_inner:d_inner + conv_dim]
    conv = cb_ref[...] + cw_ref[SSM_CONV_W - 1:SSM_CONV_W, :] * xnew
    for k in range(SSM_CONV_W - 1):
        conv = conv + cw_ref[k:k + 1, :] * buf_ref[k]
    for k in range(SSM_CONV_W - 2):
        newbuf_ref[k] = buf_ref[k + 1]
    newbuf_ref[SSM_CONV_W - 2] = xnew
    xbc = _silu(conv)
    xs = xbc[:, :d_inner]
    dt_raw = zx_ref[:, d_inner + conv_dim:d_inner + conv_dim + 128] + dtb_ref[...]
    dt = jnp.maximum(dt_raw, 0.0) + jnp.log1p(jnp.exp(-jnp.abs(dt_raw)))
    decay_ref[...] = jnp.exp(dt * (-jnp.exp(alog_ref[...])))
    xs_ref[...] = xs
    xdt_ref[...] = xs * _dot_exact_rhs(dt, expand_ref[...])
    b_ref[...] = xbc[:, d_inner:d_inner + gn]
    c_ref[...] = xbc[:, d_inner + gn:]


def _col_bcast(rowvec):
    n = rowvec.shape[1]
    wide = jnp.broadcast_to(rowvec, (128, n))
    return jnp.concatenate([jnp.transpose(wide[:, j * 128:(j + 1) * 128]) for j in range(n // 128)], axis=0)


def _ssm_state_kernel(n_heads, h0_ref, xdt_ref, decay_ref, b_ref, c_ref, hn_ref, y_ref):
    ns = SSM_D_STATE
    hpg = n_heads // SSM_GROUPS
    xcol = _col_bcast(xdt_ref[0])
    dcol = _col_bcast(decay_ref[0])
    for g in range(SSM_GROUPS):
        brow = b_ref[0, :, g * ns:(g + 1) * ns]
        crow = jnp.broadcast_to(c_ref[0, :, g * ns:(g + 1) * ns], (8, ns))
        chi, cmid, _ = _split3(crow)
        parts = []
        for r in range(hpg):
            h = g * hpg + r
            hn = (h0_ref[0, h] * dcol[h:h + 1, :]
                  + xcol[h * SSM_HEAD_DIM:(h + 1) * SSM_HEAD_DIM, :] * brow)
            hn_ref[0, h] = hn
            parts.append(hn)
        hhi, hmid, _ = _split3(jnp.concatenate(parts, axis=0))
        yr = _dot_nt(chi, hhi) + _dot_nt(cmid, hhi) + _dot_nt(chi, hmid)
        y_ref[0, :, g * hpg * SSM_HEAD_DIM:(g + 1) * hpg * SSM_HEAD_DIM] = yr[0:1, :]


def _ssm_post_kernel(d_inner, y_ref, xs_ref, zx_ref, dskip_ref, nw_ref, g_ref):
    gw = d_inner // SSM_GROUPS
    y = y_ref[...] + dskip_ref[...] * xs_ref[...]
    gated = y * _silu(zx_ref[:, 0:d_inner])
    for g in range(SSM_GROUPS):
        cols = slice(g * gw, (g + 1) * gw)
        g_ref[:, cols] = (_rms_unit(gated[:, cols]) * nw_ref[:, cols]).astype(BF16)


def ssm_step_sample(zx, conv_buf_t, h0, d_inner, cw, cb, dtb, alog, dskip_hp, nw, expand):
    bsz = zx.shape[0]
    n_heads = d_inner // SSM_HEAD_DIM
    gn = SSM_GROUPS * SSM_D_STATE
    conv_dim = d_inner + 2 * gn
    newbuf, xs, xdt, bm, cm, decay = pl.pallas_call(
        functools.partial(_ssm_prep_kernel, d_inner),
        out_shape=[jax.ShapeDtypeStruct(conv_buf_t.shape, F32),
                   jax.ShapeDtypeStruct((bsz, d_inner), F32),
                   jax.ShapeDtypeStruct((bsz, d_inner), F32),
                   jax.ShapeDtypeStruct((bsz, gn), F32),
                   jax.ShapeDtypeStruct((bsz, gn), F32),
                   jax.ShapeDtypeStruct((bsz, 128), F32)],
        compiler_params=pltpu.CompilerParams(vmem_limit_bytes=VMEM_LIMIT_BYTES),
        name="ssm_prep",
    )(zx, conv_buf_t, cw, cb, dtb, alog, expand)
    row3 = lambda a: a.reshape(bsz, 1, a.shape[1])
    hn, y = pl.pallas_call(
        functools.partial(_ssm_state_kernel, n_heads),
        grid=(bsz,),
        in_specs=[pl.BlockSpec((1, n_heads, SSM_HEAD_DIM, SSM_D_STATE), lambda b: (b, 0, 0, 0)),
                  pl.BlockSpec((1, 1, d_inner), lambda b: (b, 0, 0)),
                  pl.BlockSpec((1, 1, 128), lambda b: (b, 0, 0)),
                  pl.BlockSpec((1, 1, gn), lambda b: (b, 0, 0)),
                  pl.BlockSpec((1, 1, gn), lambda b: (b, 0, 0))],
        out_specs=[pl.BlockSpec((1, n_heads, SSM_HEAD_DIM, SSM_D_STATE), lambda b: (b, 0, 0, 0)),
                   pl.BlockSpec((1, 1, d_inner), lambda b: (b, 0, 0))],
        out_shape=[jax.ShapeDtypeStruct(h0.shape, F32),
                   jax.ShapeDtypeStruct((bsz, 1, d_inner), F32)],
        compiler_params=_cparams("parallel"),
        name="ssm_state",
    )(h0, row3(xdt), row3(decay), row3(bm), row3(cm))
    gact = pl.pallas_call(
        functools.partial(_ssm_post_kernel, d_inner),
        out_shape=jax.ShapeDtypeStruct((bsz, d_inner), BF16),
        compiler_params=pltpu.CompilerParams(vmem_limit_bytes=VMEM_LIMIT_BYTES),
        name="ssm_post",
    )(y.reshape(bsz, d_inner), xs, zx, dskip_hp, nw)
    return gact, newbuf, hn


def _ffn_gate_sample_kernel(f, gu_ref, buf_ref, cw_ref, cb_ref, h_ref, newbuf_ref):
    g = gu_ref[:, 0:f]
    u = gu_ref[:, f:2 * f]
    c = cb_ref[...] + cw_ref[FFN_CONV_W - 1:FFN_CONV_W, :] * g
    for k in range(FFN_CONV_W - 1):
        c = c + cw_ref[k:k + 1, :] * buf_ref[k]
    for k in range(FFN_CONV_W - 2):
        newbuf_ref[k] = buf_ref[k + 1]
    newbuf_ref[FFN_CONV_W - 2] = g
    h_ref[...] = (_silu(c) * u).astype(BF16)


def ffn_gate_sample(gu, buf_t, cw, cb):
    bsz = gu.shape[0]
    f = cw.shape[1]
    return pl.pallas_call(
        functools.partial(_ffn_gate_sample_kernel, f),
        out_shape=[jax.ShapeDtypeStruct((bsz, f), BF16),
                   jax.ShapeDtypeStruct(buf_t.shape, F32)],
        compiler_params=pltpu.CompilerParams(vmem_limit_bytes=VMEM_LIMIT_BYTES),
        name="ffn_gate_sample",
    )(gu, buf_t, cw, cb)


def _dec_attn_kernel(npg, lambda_init, pt_ref, q_ref, kn_ref, vn_ref, lv_ref, sub_ref, *rest):
    k_refs = rest[:npg]
    v_refs = rest[npg:2 * npg]
    o_ref = rest[2 * npg]
    s_ref, m_ref, l_ref, acc_ref = rest[2 * npg + 1:]
    del pt_ref
    j = pl.program_id(1)
    page, nh, hw = k_refs[0].shape[1:]
    hd = hw // 2
    rows = page * nh

    q8 = q_ref[0]
    lane8 = lax.broadcasted_iota(jnp.int32, (nh, hw), 1)
    zero8 = jnp.zeros_like(q8)
    qstack = jnp.concatenate([jnp.where(lane8 < hd, q8, zero8), jnp.where(lane8 >= hd, q8, zero8),
                              jnp.zeros((128 - 2 * nh, hw), BF16)], axis=0)
    lane = lax.broadcasted_iota(jnp.int32, (nh, 128), 1)
    sub = lax.broadcasted_iota(jnp.int32, (nh, 128), 0)
    keep8 = jnp.logical_and(lane < 2 * nh, lane % nh == sub)
    brow = lax.broadcasted_iota(jnp.int32, (128, 2 * hw), 0)
    bcol = lax.broadcasted_iota(jnp.int32, (128, 2 * hw), 1)
    spread = jnp.where(jnp.logical_and(brow < 2 * nh, brow // nh == bcol // hw), 1.0, 0.0).astype(BF16)

    @pl.when(j == 0)
    def _():
        s_new = _dot_nt(kn_ref[0].astype(BF16), qstack)
        m_ref[...] = jnp.where(keep8, s_new, NEG)
        l_ref[...] = jnp.where(keep8, 1.0, 0.0)
        vn = vn_ref[0]
        acc_ref[...] = jnp.concatenate([vn, vn], axis=1)

    keep = jnp.broadcast_to(keep8[None], (page, nh, 128))
    m_old = m_ref[...]
    m_new = m_old
    for p in range(npg):
        kf = k_refs[p][0].reshape(rows, hw).astype(BF16)
        s = _dot_nt(kf, qstack).reshape(page, nh, 128)
        s = jnp.where(keep, s, NEG)
        s_ref[p] = s
        m_new = jnp.maximum(m_new, jnp.max(s, axis=0))
    corr = jnp.where(keep8, jnp.exp(m_old - m_new), 0.0)
    m_ref[...] = m_new
    l_run = l_ref[...] * corr
    acc = acc_ref[...] * _dot_exact_rhs(corr, spread)
    for p in range(npg):
        pr = jnp.where(keep, jnp.exp(s_ref[p] - m_new[None]), 0.0)
        l_run = l_run + jnp.sum(pr, axis=0)
        pb = _dot(pr.reshape(rows, 128).astype(BF16), spread).reshape(page, nh, 2 * hw)
        vp = v_refs[p][0]
        acc = acc + jnp.concatenate([jnp.sum(pb[:, :, :hw] * vp, axis=0),
                                     jnp.sum(pb[:, :, hw:] * vp, axis=0)], axis=1)
    l_ref[...] = l_run
    acc_ref[...] = acc

    @pl.when(j == pl.num_programs(1) - 1)
    def _():
        o2 = acc / _dot_exact_rhs(l_run, spread)
        lam = _diff_lambda(lv_ref[...], lambda_init)
        o = o2[:, :hw] - lam * o2[:, hw:]
        o = _rms_unit(o) * sub_ref[...]
        o_ref[0] = (o * (1.0 - lambda_init)).astype(BF16)


def diff_attention_sample(q, k_new, v_new, cache_k, cache_v, page_table, lvec, subln, lambda_init):
    bsz, nh, hw = q.shape
    n_pages = page_table.shape[1]
    page = cache_k.shape[1]
    npg = min(PAGES_PER_STEP, n_pages)
    assert n_pages % npg == 0
    row = pl.BlockSpec((1, nh, hw), lambda b, j, pt: (b, 0, 0))

    def page_spec(p):
        return pl.BlockSpec((1, page, nh, hw), lambda b, j, pt: (pt[b, j * npg + p], 0, 0, 0))

    grid_spec = pltpu.PrefetchScalarGridSpec(
        num_scalar_prefetch=1,
        grid=(bsz, n_pages // npg),
        in_specs=[row, row, row,
                  pl.BlockSpec(lvec.shape, lambda b, j, pt: (0, 0)),
                  pl.BlockSpec(subln.shape, lambda b, j, pt: (0, 0))]
        + [page_spec(p) for p in range(npg)] + [page_spec(p) for p in range(npg)],
        out_specs=row,
        scratch_shapes=[pltpu.VMEM((npg, page, nh, 128), F32),
                        pltpu.VMEM((nh, 128), F32), pltpu.VMEM((nh, 128), F32),
                        pltpu.VMEM((nh, 2 * hw), F32)],
    )
    return pl.pallas_call(
        functools.partial(_dec_attn_kernel, npg, lambda_init),
        grid_spec=grid_spec,
        out_shape=jax.ShapeDtypeStruct((bsz, nh, hw), BF16),
        compiler_params=_cparams("parallel", "arbitrary"),
        name="diff_attn_sample",
    )(page_table, q, k_new, v_new, lvec, subln, *([cache_k] * npg), *([cache_v] * npg))


def kernel(x_prompt, x_sample, state_ssm, state_conv_ssm, state_conv_ffn, cache_k, cache_v, page_table,
           norm_mix, norm_ffn, ssm_in_proj, ssm_conv_w, ssm_conv_b, ssm_dt_bias, ssm_a_log, ssm_d,
           ssm_norm, ssm_out_proj, ffn_up, ffn_conv_w, ffn_conv_b, ffn_down, norm_kv, w_k, w_v,
           w_q, lambda_q1, lambda_k1, lambda_q2, lambda_k2, subln, w_o, norm_final):
    bp, seq, d = x_prompt.shape
    bd = x_sample.shape[0]
    depth = norm_mix.shape[0]
    assert depth == 2 and ssm_in_proj.shape[0] == 1 and x_sample.shape[1] == 1
    d_inner = ssm_norm.shape[1]
    n_heads = ssm_dt_bias.shape[1]
    assert n_heads * SSM_HEAD_DIM == d_inner and n_heads <= 128
    conv_dim = ssm_conv_w.shape[2]
    f = ffn_conv_w.shape[2]
    attn_w = w_k.shape[1]
    hw = attn_w // ATTN_HEADS
    mp = bp * seq

    xp = x_prompt.reshape(mp, d)
    xs = x_sample.reshape(bd, d)

    in_dim = ssm_in_proj.shape[2]
    in_pad = -(-(d_inner + conv_dim + 128) // 768) * 768
    w_in = jnp.pad(ssm_in_proj[0], ((0, 0), (0, in_pad - in_dim))).astype(BF16)
    w_out = ssm_out_proj[0].astype(BF16)
    w_up = [ffn_up[i].astype(BF16) for i in range(depth)]
    w_down = [ffn_down[i].astype(BF16) for i in range(depth)]
    wk, wv, wq, wo = w_k.astype(BF16), w_v.astype(BF16), w_q[0].astype(BF16), w_o[0].astype(BF16)
    dtb = jnp.pad(ssm_dt_bias[0], (0, 128 - n_heads)).reshape(1, 128)
    alog = jnp.pad(ssm_a_log[0], (0, 128 - n_heads)).reshape(1, 128)
    dskip_hp = jnp.repeat(ssm_d[0], SSM_HEAD_DIM).reshape(1, d_inner)
    ssm_nw = ssm_norm[0].reshape(1, d_inner)
    expand = (lax.broadcasted_iota(jnp.int32, (128, d_inner), 0)
              == lax.broadcasted_iota(jnp.int32, (128, d_inner), 1) // SSM_HEAD_DIM).astype(BF16)
    cw0, cb0 = ssm_conv_w[0], ssm_conv_b[0].reshape(1, conv_dim)
    lambda_init = 0.8 - 0.6 * math.exp(-0.3 * 1)
    lvec = jnp.concatenate([lambda_q1, lambda_k1, lambda_q2, lambda_k2], axis=0)
    sub_w = subln[0].reshape(1, hw)
    scale = (hw // 2) ** -0.5

    g0 = norm_mix[0:1]
    (zx_p,) = norm_matmul(xp, g0, [0], [w_in], [[F32]], tm=512, tn=768, name="in_proj_p")
    gact_p, tail_p, ssm_state_p = ssd_prompt(zx_p, bp, seq, d_inner, cw0, cb0, dtb, alog, dskip_hp,
                                            ssm_nw, expand)
    xp = matmul_residual(gact_p, w_out, xp, name="out_proj_p")
    ssm_conv_p = tail_p[:, 8 - (SSM_CONV_W - 1):, :]

    (zx_s,) = norm_matmul(xs, g0, [0], [w_in], [[F32]], tn=768, name="in_proj_s")
    gact_s, newbuf_s, ssm_state_s = ssm_step_sample(
        zx_s, jnp.swapaxes(state_conv_ssm[0], 0, 1), state_ssm[0], d_inner, cw0, cb0, dtb, alog,
        dskip_hp, ssm_nw, expand)
    xs = matmul_residual(gact_s, w_out, xs, name="out_proj_s")
    ssm_conv_s = jnp.swapaxes(newbuf_s, 0, 1)

    ffn_conv_p, ffn_conv_s = [], []

    def conv_ffn(i, xp, xs, final_gain):
        gain = norm_ffn[i:i + 1]
        cw, cb = ffn_conv_w[i], ffn_conv_b[i].reshape(1, f)
        tm = _row_tile(seq, 512)
        h_p, tail = ffn_up_prompt(xp, gain, w_up[i], cw, cb, seq, tm=tm, tn=f // 2, name=f"ffn_up_p{i}")
        xp = matmul_residual(h_p, w_down[i], xp, final_gain, name=f"ffn_down_p{i}")
        tiles_per_seq = seq // tm
        tail = tail.reshape(bp, tiles_per_seq, 8, f)[:, -1, 8 - (FFN_CONV_W - 1):, :]
        ffn_conv_p.append(tail)
        (gu,) = norm_matmul(xs, gain, [0], [w_up[i]], [[F32]], tn=f, name=f"ffn_up_s{i}")
        h_s, nb = ffn_gate_sample(gu, jnp.swapaxes(state_conv_ffn[i], 0, 1), cw, cb)
        xs = matmul_residual(h_s, w_down[i], xs, final_gain, name=f"ffn_down_s{i}")
        ffn_conv_s.append(jnp.swapaxes(nb, 0, 1))
        return xp, xs

    xp, xs = conv_ffn(0, xp, xs, None)

    gains = jnp.concatenate([norm_kv.reshape(1, d), norm_mix[1:2]], axis=0)
    kp, kp16, vp, vp16, qp16 = norm_matmul(
        xp, gains, [0, 0, 1], [wk, wv, wq], [[F32, BF16], [F32, BF16], [BF16]],
        scales=[1.0, 1.0, scale], tm=512, tn=attn_w, name="kvq_p")
    ks, vs, qs16 = norm_matmul(
        xs, gains, [0, 0, 1], [wk, wv, wq], [[F32], [F32], [BF16]],
        scales=[1.0, 1.0, scale], tn=attn_w, name="kvq_s")

    o_p = diff_attention_prompt(qp16.reshape(bp, seq, attn_w), kp16.reshape(bp, seq, attn_w),
                                vp16.reshape(bp, seq, attn_w), lvec, sub_w, lambda_init)
    xp = matmul_residual(o_p.reshape(mp, attn_w), wo, xp, name="attn_out_p")
    o_s = diff_attention_sample(qs16.reshape(bd, ATTN_HEADS, hw), ks.reshape(bd, ATTN_HEADS, hw),
                                vs.reshape(bd, ATTN_HEADS, hw), cache_k, cache_v, page_table,
                                lvec, sub_w, lambda_init)
    xs = matmul_residual(o_s.reshape(bd, attn_w), wo, xs, name="attn_out_s")

    xp, xs = conv_ffn(1, xp, xs, norm_final.reshape(1, d))

    return (xp.reshape(bp, seq, d), xs.reshape(bd, 1, d),
            ssm_state_p[None], ssm_state_s[None], ssm_conv_p[None], ssm_conv_s[None],
            jnp.stack(ffn_conv_p), jnp.stack(ffn_conv_s),
            kp.reshape(bp, seq, ATTN_HEADS, hw), vp.reshape(bp, seq, ATTN_HEADS, hw),
            ks.reshape(bd, 1, ATTN_HEADS, hw), vs.reshape(bd, 1, ATTN_HEADS, hw))
```

```python
import functools
import math

import jax
import jax.numpy as jnp
from jax import lax
from jax.experimental import pallas as pl
from jax.experimental.pallas import tpu as pltpu

F32 = jnp.float32
BF16 = jnp.bfloat16
EPS = 1e-5
LOG2E = 1.4426950408889634

SSM_HEAD_DIM = 64
SSM_GROUPS = 4
SSM_D_STATE = 128
SSM_CONV_W = 4
SSD_CHUNK = 128
ATTN_HEADS = 8
FFN_CONV_W = 3
PAGES_PER_STEP = 8
DEC_GROUP_PAGES = 2
DEC_SCORE_CHUNK = 16
DEC_PROB_CHUNK = 8

VMEM_LIMIT_BYTES = 56 * 1024 * 1024


def _cparams(*sem):
    return pltpu.CompilerParams(dimension_semantics=sem, vmem_limit_bytes=VMEM_LIMIT_BYTES)


def _silu(x):
    return x / (1.0 + jnp.exp(-x))


def _rms_unit(x):
    return x * lax.rsqrt(jnp.mean(x * x, axis=-1, keepdims=True) + EPS)


def _split3(a):
    hi = a.astype(BF16)
    r = a - hi.astype(F32)
    mid = r.astype(BF16)
    lo = (r - mid.astype(F32)).astype(BF16)
    return hi, mid, lo


def _dot(a, b):
    return jnp.dot(a, b, preferred_element_type=F32)


def _dot_nt(a, b):
    return lax.dot_general(a, b, (((1,), (1,)), ((), ())), preferred_element_type=F32)


def _dot_exact_rhs(a, b_bf16):
    hi, mid, lo = _split3(a)
    return _dot(hi, b_bf16) + _dot(mid, b_bf16) + _dot(lo, b_bf16)


def _dot_exact_lhs(a_bf16, b):
    hi, mid, lo = _split3(b)
    return _dot(a_bf16, hi) + _dot(a_bf16, mid) + _dot(a_bf16, lo)


def _row_tile(m, pref):
    t = min(m, pref)
    assert m % t == 0, (m, t)
    return t


def _norm_mm_kernel(n_w, gain_idx, scales, out_dtypes, x_ref, g_ref, *rest):
    w_refs = rest[:n_w]
    o_refs = rest[n_w:-1]
    xn_ref = rest[-1]

    @pl.when(pl.program_id(1) == 0)
    def _():
        xh = _rms_unit(x_ref[...])
        for i in range(n_w):
            gi = gain_idx[i]
            xn_ref[i] = (xh * g_ref[gi:gi + 1, :]).astype(BF16)

    k = 0
    for i in range(n_w):
        r = _dot(xn_ref[i], w_refs[i][...])
        if scales[i] != 1.0:
            r = r * scales[i]
        for dt in out_dtypes[i]:
            o_refs[k][...] = r.astype(dt)
            k += 1


def norm_matmul(x, gains, gain_idx, weights, out_dtypes, scales=None, tm=512, tn=None, name="norm_mm"):
    m, kdim = x.shape
    n = weights[0].shape[1]
    n_w = len(weights)
    scales = tuple(scales) if scales is not None else (1.0,) * n_w
    tm = _row_tile(m, tm)
    tn = n if tn is None else tn
    assert n % tn == 0
    outs, specs = [], []
    for dts in out_dtypes:
        for dt in dts:
            outs.append(jax.ShapeDtypeStruct((m, n), dt))
            specs.append(pl.BlockSpec((tm, tn), lambda i, j: (i, j)))
    kern = functools.partial(_norm_mm_kernel, n_w, tuple(gain_idx), scales,
                             tuple(tuple(d) for d in out_dtypes))
    return pl.pallas_call(
        kern,
        grid=(m // tm, n // tn),
        in_specs=[pl.BlockSpec((tm, kdim), lambda i, j: (i, 0)),
                  pl.BlockSpec(gains.shape, lambda i, j: (0, 0))]
        + [pl.BlockSpec((kdim, tn), lambda i, j: (0, j)) for _ in weights],
        out_specs=specs,
        out_shape=outs,
        scratch_shapes=[pltpu.VMEM((n_w, tm, kdim), BF16)],
        compiler_params=_cparams("parallel", "arbitrary"),
        name=name,
    )(x, gains, *weights)


def _norm_mm_resident_kernel(tn, x_ref, g_ref, w_ref, o_ref):
    xn = (_rms_unit(x_ref[...]) * g_ref[...]).astype(BF16)
    for j in range(w_ref.shape[1] // tn):
        o_ref[:, j * tn:(j + 1) * tn] = _dot(xn, w_ref[:, j * tn:(j + 1) * tn])


def norm_matmul_resident(x, gain, w, tm=512, tn=768, name="norm_mm_res"):
    m, kdim = x.shape
    n = w.shape[1]
    tm = _row_tile(m, tm)
    assert n % tn == 0
    return pl.pallas_call(
        functools.partial(_norm_mm_resident_kernel, tn),
        grid=(m // tm,),
        in_specs=[pl.BlockSpec((tm, kdim), lambda i: (i, 0)),
                  pl.BlockSpec((1, kdim), lambda i: (0, 0)),
                  pl.BlockSpec((kdim, n), lambda i: (0, 0), pipeline_mode=pl.Buffered(1))],
        out_specs=pl.BlockSpec((tm, n), lambda i: (i, 0)),
        out_shape=jax.ShapeDtypeStruct((m, n), F32),
        compiler_params=_cparams("parallel"),
        name=name,
    )(x, gain, w)


def _mm_res_kernel(final_norm, a_ref, w_ref, r_ref, *rest):
    o_ref = rest[-1]
    y = r_ref[...] + _dot(a_ref[...], w_ref[...])
    if final_norm:
        y = _rms_unit(y) * rest[0][...]
    o_ref[...] = y


def matmul_residual(a, w, res, final_gain=None, tm=512, name="mm_res"):
    m, kdim = a.shape
    n = w.shape[1]
    tm = _row_tile(m, tm)
    ins = [a, w, res]
    specs = [pl.BlockSpec((tm, kdim), lambda i: (i, 0)),
             pl.BlockSpec((kdim, n), lambda i: (0, 0)),
             pl.BlockSpec((tm, n), lambda i: (i, 0))]
    if final_gain is not None:
        ins.append(final_gain)
        specs.append(pl.BlockSpec((1, n), lambda i: (0, 0)))
    return pl.pallas_call(
        functools.partial(_mm_res_kernel, final_gain is not None),
        grid=(m // tm,),
        in_specs=specs,
        out_specs=pl.BlockSpec((tm, n), lambda i: (i, 0)),
        out_shape=jax.ShapeDtypeStruct((m, n), F32),
        compiler_params=_cparams("parallel"),
        name=name,
    )(*ins)


HALO = 16


def _ffn_up_kernel(tm, seq, x_ref, halo_ref, g_ref, wg_ref, wu_ref, cw_ref, cb_ref,
                   h_ref, tail_ref, xn_ref):
    i = pl.program_id(0)

    @pl.when(pl.program_id(1) == 0)
    def _():
        gain = g_ref[...]
        xn_ref[HALO:, :] = (_rms_unit(x_ref[...]) * gain).astype(BF16)
        keep = jnp.where((i * tm) % seq == 0, 0.0, 1.0)
        xn_ref[0:HALO, :] = (_rms_unit(halo_ref[...]) * gain * keep).astype(BF16)

    g = _dot(xn_ref[...], wg_ref[...])
    u = _dot(xn_ref[HALO:, :], wu_ref[...])
    c = (cw_ref[2:3, :] * g + cw_ref[1:2, :] * pltpu.roll(g, 1, 0)
         + cw_ref[0:1, :] * pltpu.roll(g, 2, 0) + cb_ref[...])
    c = c[HALO:, :]
    h_ref[...] = (_silu(c) * u).astype(BF16)
    tail_ref[0] = g[tm + HALO - 8:, :]


def ffn_up_prompt(x, gain, w_up, cw, cb, seq, tm=512, tn=None, name="ffn_up"):
    m, d = x.shape
    f = cw.shape[1]
    tm = _row_tile(seq, tm)
    tn = f if tn is None else tn
    assert f % tn == 0 and tm % HALO == 0
    nj = f // tn
    hb = tm // HALO
    h, tail = pl.pallas_call(
        functools.partial(_ffn_up_kernel, tm, seq),
        grid=(m // tm, nj),
        in_specs=[pl.BlockSpec((tm, d), lambda i, j: (i, 0)),
                  pl.BlockSpec((HALO, d), lambda i, j: (jnp.maximum(i * hb - 1, 0), 0)),
                  pl.BlockSpec((1, d), lambda i, j: (0, 0)),
                  pl.BlockSpec((d, tn), lambda i, j: (0, j)),
                  pl.BlockSpec((d, tn), lambda i, j: (0, nj + j)),
                  pl.BlockSpec((FFN_CONV_W, tn), lambda i, j: (0, j)),
                  pl.BlockSpec((1, tn), lambda i, j: (0, j))],
        out_specs=[pl.BlockSpec((tm, tn), lambda i, j: (i, j)),
                   pl.BlockSpec((1, 8, tn), lambda i, j: (i, 0, j))],
        out_shape=[jax.ShapeDtypeStruct((m, f), BF16),
                   jax.ShapeDtypeStruct((m // tm, 8, f), F32)],
        scratch_shapes=[pltpu.VMEM((HALO + tm, d), BF16)],
        compiler_params=_cparams("parallel", "arbitrary"),
        name=name,
    )(x, x, gain, w_up, w_up, cw, cb)
    return h, tail


def _ssd_kernel(d_inner, n_heads, zx_ref, cw_ref, cb_ref, dtb_ref, alog_ref, dskip_ref, nw_ref,
                expand_ref, g_ref, tail_ref, state_ref, xpad_ref, ht_ref, y_ref):
    q = SSD_CHUNK
    ns = SSM_D_STATE
    gn = SSM_GROUPS * ns
    conv_dim = d_inner + 2 * gn
    gw = d_inner // SSM_GROUPS
    c = pl.program_id(1)
    last = pl.num_programs(1) - 1

    @pl.when(c == 0)
    def _():
        xpad_ref[0:8, :] = jnp.zeros((8, conv_dim), F32)
        ht_ref[...] = jnp.zeros(ht_ref.shape, F32)

    xpad_ref[8:, :] = zx_ref[:, d_inner:d_inner + conv_dim]
    conv = cb_ref[...] + cw_ref[SSM_CONV_W - 1:SSM_CONV_W, :] * xpad_ref[8:, :]
    for k in range(1, SSM_CONV_W):
        conv = conv + cw_ref[SSM_CONV_W - 1 - k:SSM_CONV_W - k, :] * xpad_ref[pl.ds(8 - k, q), :]
    new_tail = xpad_ref[q:q + 8, :]
    xpad_ref[0:8, :] = new_tail
    xbc = _silu(conv)
    xs = xbc[:, :d_inner]
    bm = xbc[:, d_inner:d_inner + gn]
    cm = xbc[:, d_inner + gn:]

    dt_raw = zx_ref[:, d_inner + conv_dim:d_inner + conv_dim + 128] + dtb_ref[...]
    dt = jnp.maximum(dt_raw, 0.0) + jnp.log1p(jnp.exp(-jnp.abs(dt_raw)))
    da = dt * (-jnp.exp(alog_ref[...]))
    row = lax.broadcasted_iota(jnp.int32, (q, q), 0)
    col = lax.broadcasted_iota(jnp.int32, (q, q), 1)
    tri = row >= col
    a_cum = _dot_exact_lhs(jnp.where(tri, 1.0, 0.0).astype(BF16), da)
    a_cum_t = jnp.transpose(a_cum)

    expand = expand_ref[...]
    dt_hp = _dot_exact_rhs(dt, expand)
    a_hp = _dot_exact_rhs(a_cum, expand)
    xdt = xs * dt_hp
    a_last = a_hp[q - 1:q, :]
    xdt_b = xdt.astype(BF16)
    dec_x = (jnp.exp(a_last - a_hp) * xdt).astype(BF16)
    off_scale = jnp.exp(a_hp)
    chunk_decay = jnp.exp(a_last)

    lane = lax.broadcasted_iota(jnp.int32, (q, 2 * SSM_HEAD_DIM), 1)
    heads_per_group = n_heads // SSM_GROUPS
    for g in range(SSM_GROUPS):
        bg = bm[:, g * ns:(g + 1) * ns]
        cg = cm[:, g * ns:(g + 1) * ns].astype(BF16)
        cb_mat = _dot_nt(cg, bg.astype(BF16))
        cols = slice(g * gw, (g + 1) * gw)
        ht = ht_ref[g]
        y_off = _dot(cg, ht.astype(BF16)) * off_scale[:, cols]
        new_states = _dot(jnp.transpose(bg).astype(BF16), dec_x[:, cols])
        ht_ref[g] = ht * chunk_decay[:, cols] + new_states
        for pr in range(heads_per_group // 2):
            c0 = g * gw + pr * 2 * SSM_HEAD_DIM
            x2 = xdt_b[:, c0:c0 + 2 * SSM_HEAD_DIM]
            yd = None
            for half in range(2):
                hh = g * heads_per_group + pr * 2 + half
                seg = a_cum[:, hh:hh + 1] - a_cum_t[hh:hh + 1, :]
                lmat = jnp.where(tri, jnp.exp(jnp.minimum(seg, 0.0)), 0.0)
                mm = (cb_mat * lmat).astype(BF16)
                xh = jnp.where((lane >= SSM_HEAD_DIM) == (half == 1), x2, jnp.zeros_like(x2))
                t = _dot(mm, xh)
                yd = t if yd is None else yd + t
            y_ref[:, c0:c0 + 2 * SSM_HEAD_DIM] = yd + y_off[:, pr * 2 * SSM_HEAD_DIM:(pr + 1) * 2 * SSM_HEAD_DIM]

    y = y_ref[...] + dskip_ref[...] * xs
    gated = y * _silu(zx_ref[:, 0:d_inner])
    for g in range(SSM_GROUPS):
        cols = slice(g * gw, (g + 1) * gw)
        g_ref[:, cols] = (_rms_unit(gated[:, cols]) * nw_ref[:, cols]).astype(BF16)

    @pl.when(c == last)
    def _():
        tail_ref[0] = new_tail
        for g in range(SSM_GROUPS):
            st = jnp.transpose(ht_ref[g])
            state_ref[0, g * heads_per_group:(g + 1) * heads_per_group] = st.reshape(
                heads_per_group, SSM_HEAD_DIM, ns)


def ssd_prompt(zx, batch, seq, d_inner, cw, cb, dtb, alog, dskip_hp, nw, expand, name="ssd"):
    m, width = zx.shape
    n_heads = d_inner // SSM_HEAD_DIM
    conv_dim = cw.shape[1]
    nc = seq // SSD_CHUNK
    q = SSD_CHUNK
    full = lambda shape: pl.BlockSpec(shape, lambda b, c: (0,) * len(shape))
    return pl.pallas_call(
        functools.partial(_ssd_kernel, d_inner, n_heads),
        grid=(batch, nc),
        in_specs=[pl.BlockSpec((q, width), lambda b, c: (b * nc + c, 0)),
                  full(cw.shape), full(cb.shape), full(dtb.shape), full(alog.shape),
                  full(dskip_hp.shape), full(nw.shape), full(expand.shape)],
        out_specs=[pl.BlockSpec((q, d_inner), lambda b, c: (b * nc + c, 0)),
                   pl.BlockSpec((1, 8, conv_dim), lambda b, c: (b, 0, 0)),
                   pl.BlockSpec((1, n_heads, SSM_HEAD_DIM, SSM_D_STATE), lambda b, c: (b, 0, 0, 0))],
        out_shape=[jax.ShapeDtypeStruct((m, d_inner), BF16),
                   jax.ShapeDtypeStruct((batch, 8, conv_dim), F32),
                   jax.ShapeDtypeStruct((batch, n_heads, SSM_HEAD_DIM, SSM_D_STATE), F32)],
        scratch_shapes=[pltpu.VMEM((q + 8, conv_dim), F32),
                        pltpu.VMEM((SSM_GROUPS, SSM_D_STATE, d_inner // SSM_GROUPS), F32),
                        pltpu.VMEM((q, d_inner), F32)],
        compiler_params=_cparams("parallel", "arbitrary"),
        name=name,
    )(zx, cw, cb, dtb, alog, dskip_hp, nw, expand)


def _diff_lambda(lv, lambda_init):
    s1 = jnp.sum(lv[0:1, :] * lv[1:2, :], axis=-1, keepdims=True)
    s2 = jnp.sum(lv[2:3, :] * lv[3:4, :], axis=-1, keepdims=True)
    return jnp.exp(s1) - jnp.exp(s2) + lambda_init


ATTN_ROW_CHUNK = 64


def _attn_kernel(tq, lambda_init, q_ref, k_ref, v_ref, lv_ref, sub_ref, o_ref,
                 q2_ref, s0_ref, s1_ref, p_ref, m_ref, c_ref, acc_ref):
    qi = pl.program_id(2)
    hw = q_ref.shape[2]
    hd = hw // 2
    rc = ATTN_ROW_CHUNK
    qh = q_ref[0]
    lane = lax.broadcasted_iota(jnp.int32, qh.shape, 1)
    zero = jnp.zeros_like(qh)
    q2_ref[0:tq, :] = jnp.where(lane < hd, qh, zero)
    q2_ref[tq:, :] = jnp.where(lane >= hd, qh, zero)
    m_ref[...] = jnp.full(m_ref.shape, -jnp.inf, F32)
    acc_ref[...] = jnp.zeros(acc_ref.shape, F32)
    ones = jnp.ones((tq, hw), BF16)

    def qk(kb, s_ref):
        start = pl.multiple_of(kb * tq, tq)
        s_ref[...] = _dot_nt(q2_ref[...], k_ref[0, pl.ds(start, tq), :])

    def softmax_pv(kb, s_ref, masked):
        start = pl.multiple_of(kb * tq, tq)
        for r0 in range(0, 2 * tq, rc):
            rows = slice(r0, r0 + rc)
            s = s_ref[rows, :]
            if masked:
                r = lax.broadcasted_iota(jnp.int32, (rc, tq), 0) + (r0 % tq)
                cidx = lax.broadcasted_iota(jnp.int32, (rc, tq), 1)
                s = jnp.where(r >= cidx, s, -jnp.inf)
            m_old = m_ref[rows, :]
            m_new = jnp.maximum(m_old, jnp.max(s, axis=-1, keepdims=True))
            m_ref[rows, :] = m_new
            c_ref[rows, :] = jnp.exp(m_old - m_new)
            p_ref[rows, :] = jnp.exp(s - m_new).astype(BF16)
        v_ext = jnp.concatenate([v_ref[0, pl.ds(start, tq), :], ones], axis=1)
        pv = _dot(p_ref[...], v_ext)
        for r0 in range(0, 2 * tq, rc):
            rows = slice(r0, r0 + rc)
            acc_ref[rows, :] = acc_ref[rows, :] * c_ref[rows, :] + pv[rows, :]

    qk(0, s0_ref)

    def body(kk, carry):
        kb = 2 * kk
        qk(kb + 1, s1_ref)
        softmax_pv(kb, s0_ref, False)
        qk(kb + 2, s0_ref)
        softmax_pv(kb + 1, s1_ref, False)
        return carry

    lax.fori_loop(0, qi // 2, body, 0)

    @pl.when(qi % 2 == 0)
    def _():
        softmax_pv(qi, s0_ref, True)

    @pl.when(qi % 2 == 1)
    def _():
        qk(qi, s1_ref)
        softmax_pv(qi - 1, s0_ref, False)
        softmax_pv(qi, s1_ref, True)

    acc = acc_ref[...]
    o = acc[:, :hw] / acc[:, hw:]
    lam = _diff_lambda(lv_ref[...], lambda_init)
    o = o[:tq] - lam * o[tq:]
    o = _rms_unit(o) * sub_ref[...]
    o_ref[0] = (o * (1.0 - lambda_init)).astype(BF16)


def diff_attention_prompt(q, k, v, lvec, subln, lambda_init, tq=512, name="diff_attn"):
    b, l, width = q.shape
    hw = width // ATTN_HEADS
    tq = _row_tile(l, tq)
    assert tq % ATTN_ROW_CHUNK == 0
    return pl.pallas_call(
        functools.partial(_attn_kernel, tq, lambda_init),
        grid=(b, ATTN_HEADS, l // tq),
        in_specs=[pl.BlockSpec((1, tq, hw), lambda bi, h, i: (bi, i, h)),
                  pl.BlockSpec((1, l, hw), lambda bi, h, i: (bi, 0, h)),
                  pl.BlockSpec((1, l, hw), lambda bi, h, i: (bi, 0, h)),
                  pl.BlockSpec(lvec.shape, lambda bi, h, i: (0, 0)),
                  pl.BlockSpec(subln.shape, lambda bi, h, i: (0, 0))],
        out_specs=pl.BlockSpec((1, tq, hw), lambda bi, h, i: (bi, i, h)),
        out_shape=jax.ShapeDtypeStruct((b, l, width), BF16),
        scratch_shapes=[pltpu.VMEM((2 * tq, hw), BF16),
                        pltpu.VMEM((2 * tq, tq), F32),
                        pltpu.VMEM((2 * tq, tq), F32),
                        pltpu.VMEM((2 * tq, tq), BF16),
                        pltpu.VMEM((2 * tq, 1), F32),
                        pltpu.VMEM((2 * tq, 1), F32),
                        pltpu.VMEM((2 * tq, 2 * hw), F32)],
        compiler_params=_cparams("parallel", "parallel", "arbitrary"),
        name=name,
    )(q, k, v, lvec, subln)


def _ssm_prep_kernel(d_inner, zx_ref, buf_ref, cw_ref, cb_ref, dtb_ref, alog_ref, expand_ref,
                     newbuf_ref, xs_ref, xdt_ref, b_ref, c_ref, decay_ref):
    gn = SSM_GROUPS * SSM_D_STATE
    conv_dim = d_inner + 2 * gn
    xnew = zx_ref[:, d_inner:d_inner + conv_dim]
    conv = cb_ref[...] + cw_ref[SSM_CONV_W - 1:SSM_CONV_W, :] * xnew
    for k in range(SSM_CONV_W - 1):
        conv = conv + cw_ref[k:k + 1, :] * buf_ref[k]
    for k in range(SSM_CONV_W - 2):
        newbuf_ref[k] = buf_ref[k + 1]
    newbuf_ref[SSM_CONV_W - 2] = xnew
    xbc = _silu(conv)
    xs = xbc[:, :d_inner]
    dt_raw = zx_ref[:, d_inner + conv_dim:d_inner + conv_dim + 128] + dtb_ref[...]
    dt = jnp.maximum(dt_raw, 0.0) + jnp.log1p(jnp.exp(-jnp.abs(dt_raw)))
    decay_ref[...] = jnp.exp(dt * (-jnp.exp(alog_ref[...])))
    xs_ref[...] = xs
    xdt_ref[...] = xs * _dot_exact_rhs(dt, expand_ref[...])
    b_ref[...] = xbc[:, d_inner:d_inner + gn]
    c_ref[...] = xbc[:, d_inner + gn:]


def _col_bcast(rowvec):
    n = rowvec.shape[1]
    wide = jnp.broadcast_to(rowvec, (128, n))
    return jnp.concatenate([jnp.transpose(wide[:, j * 128:(j + 1) * 128]) for j in range(n // 128)], axis=0)


def _ssm_state_kernel(n_heads, h0_ref, xdt_ref, decay_ref, b_ref, c_ref, hn_ref, y_ref):
    ns = SSM_D_STATE
    hpg = n_heads // SSM_GROUPS
    xcol = _col_bcast(xdt_ref[0])
    dcol = _col_bcast(decay_ref[0])
    for g in range(SSM_GROUPS):
        brow = b_ref[0, :, g * ns:(g + 1) * ns]
        crow = jnp.broadcast_to(c_ref[0, :, g * ns:(g + 1) * ns], (8, ns))
        chi, cmid, _ = _split3(crow)
        parts = []
        for r in range(hpg):
            h = g * hpg + r
            hn = (h0_ref[0, h] * dcol[h:h + 1, :]
                  + xcol[h * SSM_HEAD_DIM:(h + 1) * SSM_HEAD_DIM, :] * brow)
            hn_ref[0, h] = hn
            parts.append(hn)
        hhi, hmid, _ = _split3(jnp.concatenate(parts, axis=0))
        yr = _dot_nt(chi, hhi) + _dot_nt(cmid, hhi) + _dot_nt(chi, hmid)
        y_ref[0, :, g * hpg * SSM_HEAD_DIM:(g + 1) * hpg * SSM_HEAD_DIM] = yr[0:1, :]


def _ssm_post_kernel(d_inner, y_ref, xs_ref, zx_ref, dskip_ref, nw_ref, g_ref):
    gw = d_inner // SSM_GROUPS
    y = y_ref[...] + dskip_ref[...] * xs_ref[...]
    gated = y * _silu(zx_ref[:, 0:d_inner])
    for g in range(SSM_GROUPS):
        cols = slice(g * gw, (g + 1) * gw)
        g_ref[:, cols] = (_rms_unit(gated[:, cols]) * nw_ref[:, cols]).astype(BF16)


def ssm_step_sample(zx, conv_buf_t, h0, d_inner, cw, cb, dtb, alog, dskip_hp, nw, expand):
    bsz = zx.shape[0]
    n_heads = d_inner // SSM_HEAD_DIM
    gn = SSM_GROUPS * SSM_D_STATE
    conv_dim = d_inner + 2 * gn
    newbuf, xs, xdt, bm, cm, decay = pl.pallas_call(
        functools.partial(_ssm_prep_kernel, d_inner),
        out_shape=[jax.ShapeDtypeStruct(conv_buf_t.shape, F32),
                   jax.ShapeDtypeStruct((bsz, d_inner), F32),
                   jax.ShapeDtypeStruct((bsz, d_inner), F32),
                   jax.ShapeDtypeStruct((bsz, gn), F32),
                   jax.ShapeDtypeStruct((bsz, gn), F32),
                   jax.ShapeDtypeStruct((bsz, 128), F32)],
        compiler_params=pltpu.CompilerParams(vmem_limit_bytes=VMEM_LIMIT_BYTES),
        name="ssm_prep",
    )(zx, conv_buf_t, cw, cb, dtb, alog, expand)
    row3 = lambda a: a.reshape(bsz, 1, a.shape[1])
    hn, y = pl.pallas_call(
        functools.partial(_ssm_state_kernel, n_heads),
        grid=(bsz,),
        in_specs=[pl.BlockSpec((1, n_heads, SSM_HEAD_DIM, SSM_D_STATE), lambda b: (b, 0, 0, 0)),
                  pl.BlockSpec((1, 1, d_inner), lambda b: (b, 0, 0)),
                  pl.BlockSpec((1, 1, 128), lambda b: (b, 0, 0)),
                  pl.BlockSpec((1, 1, gn), lambda b: (b, 0, 0)),
                  pl.BlockSpec((1, 1, gn), lambda b: (b, 0, 0))],
        out_specs=[pl.BlockSpec((1, n_heads, SSM_HEAD_DIM, SSM_D_STATE), lambda b: (b, 0, 0, 0)),
                   pl.BlockSpec((1, 1, d_inner), lambda b: (b, 0, 0))],
        out_shape=[jax.ShapeDtypeStruct(h0.shape, F32),
                   jax.ShapeDtypeStruct((bsz, 1, d_inner), F32)],
        compiler_params=_cparams("parallel"),
        name="ssm_state",
    )(h0, row3(xdt), row3(decay), row3(bm), row3(cm))
    gact = pl.pallas_call(
        functools.partial(_ssm_post_kernel, d_inner),
        out_shape=jax.ShapeDtypeStruct((bsz, d_inner), BF16),
        compiler_params=pltpu.CompilerParams(vmem_limit_bytes=VMEM_LIMIT_BYTES),
        name="ssm_post",
    )(y.reshape(bsz, d_inner), xs, zx, dskip_hp, nw)
    return gact, newbuf, hn


def _ffn_gate_sample_kernel(f, gu_ref, buf_ref, cw_ref, cb_ref, h_ref, newbuf_ref):
    g = gu_ref[:, 0:f]
    u = gu_ref[:, f:2 * f]
    c = cb_ref[...] + cw_ref[FFN_CONV_W - 1:FFN_CONV_W, :] * g
    for k in range(FFN_CONV_W - 1):
        c = c + cw_ref[k:k + 1, :] * buf_ref[k]
    for k in range(FFN_CONV_W - 2):
        newbuf_ref[k] = buf_ref[k + 1]
    newbuf_ref[FFN_CONV_W - 2] = g
    h_ref[...] = (_silu(c) * u).astype(BF16)


def ffn_gate_sample(gu, buf_t, cw, cb):
    bsz = gu.shape[0]
    f = cw.shape[1]
    return pl.pallas_call(
        functools.partial(_ffn_gate_sample_kernel, f),
        out_shape=[jax.ShapeDtypeStruct((bsz, f), BF16),
                   jax.ShapeDtypeStruct(buf_t.shape, F32)],
        compiler_params=pltpu.CompilerParams(vmem_limit_bytes=VMEM_LIMIT_BYTES),
        name="ffn_gate_sample",
    )(gu, buf_t, cw, cb)


def _dec_attn_kernel(npg, lambda_init, pt_ref, q_ref, kn_ref, vn_ref, lv_ref, sub_ref, *rest):
    k_refs = rest[:npg]
    v_refs = rest[npg:2 * npg]
    o_ref = rest[2 * npg]
    s_ref, m_ref, l_ref, acc_ref = rest[2 * npg + 1:]
    del pt_ref
    j = pl.program_id(1)
    page, nh, hw = k_refs[0].shape[1:]
    hd = hw // 2
    sc, pc = DEC_SCORE_CHUNK, DEC_PROB_CHUNK

    qv = q_ref[0].astype(F32) * LOG2E
    srow = lax.broadcasted_iota(jnp.int32, (hw, 2 * hw), 0)
    scol = lax.broadcasted_iota(jnp.int32, (hw, 2 * hw), 1)
    seg = jnp.where(srow // hd == scol // hw, 1.0, 0.0).astype(BF16)

    def scores(k_tok):
        n_tok = k_tok.shape[0]
        prod = (k_tok * qv[None]).reshape(n_tok * nh, hw).astype(BF16)
        return _dot(prod, seg).reshape(n_tok, nh, 2 * hw)

    @pl.when(j == 0)
    def _():
        m_ref[...] = scores(kn_ref[...])[0]
        l_ref[...] = jnp.ones(l_ref.shape, F32)
        vn = vn_ref[0]
        acc_ref[...] = jnp.concatenate([vn, vn], axis=1)

    gp = DEC_GROUP_PAGES
    n_groups = npg // gp

    def score_chunks(g):
        return [(p, t0) for p in range(g * gp, (g + 1) * gp) for t0 in range(0, page, sc)]

    def prob_chunks(g):
        return [(p, t0) for p in range(g * gp, (g + 1) * gp) for t0 in range(0, page, pc)]

    def score_chunk(p, t0, m_run):
        s = scores(k_refs[p][0, t0:t0 + sc])
        s_ref[p, t0:t0 + sc] = s
        return jnp.maximum(m_run, jnp.max(s, axis=0))

    m_cur = m_ref[...]
    l_run = l_ref[...]
    acc = acc_ref[...]
    acc0, acc1 = acc[:, :hw], acc[:, hw:]
    m_next = m_cur
    for p, t0 in score_chunks(0):
        m_next = score_chunk(p, t0, m_next)
    for g in range(n_groups):
        m_new = m_next
        corr = jnp.exp2(m_cur - m_new)
        l_run = l_run * corr
        acc0 = acc0 * corr[:, :hw]
        acc1 = acc1 * corr[:, hw:]
        m_cur = m_new
        pending = score_chunks(g + 1) if g + 1 < n_groups else []
        every = sc // pc
        for i, (p, t0) in enumerate(prob_chunks(g)):
            if pending and i % every == 0:
                m_next = score_chunk(*pending.pop(0), m_next)
            pr = jnp.exp2(s_ref[p, t0:t0 + pc] - m_new[None])
            l_run = l_run + jnp.sum(pr, axis=0)
            vp = v_refs[p][0, t0:t0 + pc]
            acc0 = acc0 + jnp.sum(pr[:, :, :hw] * vp, axis=0)
            acc1 = acc1 + jnp.sum(pr[:, :, hw:] * vp, axis=0)
        assert not pending
    acc = jnp.concatenate([acc0, acc1], axis=1)
    m_ref[...] = m_cur
    l_ref[...] = l_run
    acc_ref[...] = acc

    @pl.when(j == pl.num_programs(1) - 1)
    def _():
        o2 = acc / l_run
        lam = _diff_lambda(lv_ref[...], lambda_init)
        o = o2[:, :hw] - lam * o2[:, hw:]
        o = _rms_unit(o) * sub_ref[...]
        o_ref[0] = (o * (1.0 - lambda_init)).astype(BF16)


def diff_attention_sample(q, k_new, v_new, cache_k, cache_v, page_table, lvec, subln, lambda_init):
    bsz, nh, hw = q.shape
    n_pages = page_table.shape[1]
    page = cache_k.shape[1]
    npg = min(PAGES_PER_STEP, n_pages)
    assert n_pages % npg == 0
    row = pl.BlockSpec((1, nh, hw), lambda b, j, pt: (b, 0, 0))

    def page_spec(p):
        return pl.BlockSpec((1, page, nh, hw), lambda b, j, pt: (pt[b, j * npg + p], 0, 0, 0))

    grid_spec = pltpu.PrefetchScalarGridSpec(
        num_scalar_prefetch=1,
        grid=(bsz, n_pages // npg),
        in_specs=[row, row, row,
                  pl.BlockSpec(lvec.shape, lambda b, j, pt: (0, 0)),
                  pl.BlockSpec(subln.shape, lambda b, j, pt: (0, 0))]
        + [page_spec(p) for p in range(npg)] + [page_spec(p) for p in range(npg)],
        out_specs=row,
        scratch_shapes=[pltpu.VMEM((npg, page, nh, 2 * hw), F32),
                        pltpu.VMEM((nh, 2 * hw), F32),
                        pltpu.VMEM((nh, 2 * hw), F32),
                        pltpu.VMEM((nh, 2 * hw), F32)],
    )
    return pl.pallas_call(
        functools.partial(_dec_attn_kernel, npg, lambda_init),
        grid_spec=grid_spec,
        out_shape=jax.ShapeDtypeStruct((bsz, nh, hw), BF16),
        compiler_params=_cparams("parallel", "arbitrary"),
        name="diff_attn_sample",
    )(page_table, q, k_new, v_new, lvec, subln, *([cache_k] * npg), *([cache_v] * npg))


def kernel(x_prompt, x_sample, state_ssm, state_conv_ssm, state_conv_ffn, cache_k, cache_v, page_table,
           norm_mix, norm_ffn, ssm_in_proj, ssm_conv_w, ssm_conv_b, ssm_dt_bias, ssm_a_log, ssm_d,
           ssm_norm, ssm_out_proj, ffn_up, ffn_conv_w, ffn_conv_b, ffn_down, norm_kv, w_k, w_v,
           w_q, lambda_q1, lambda_k1, lambda_q2, lambda_k2, subln, w_o, norm_final):
    bp, seq, d = x_prompt.shape
    bd = x_sample.shape[0]
    depth = norm_mix.shape[0]
    assert depth == 2 and ssm_in_proj.shape[0] == 1 and x_sample.shape[1] == 1
    d_inner = ssm_norm.shape[1]
    n_heads = ssm_dt_bias.shape[1]
    assert n_heads * SSM_HEAD_DIM == d_inner and n_heads <= 128
    conv_dim = ssm_conv_w.shape[2]
    f = ffn_conv_w.shape[2]
    attn_w = w_k.shape[1]
    hw = attn_w // ATTN_HEADS
    mp = bp * seq

    xp = x_prompt.reshape(mp, d)
    xs = x_sample.reshape(bd, d)

    in_dim = ssm_in_proj.shape[2]
    in_pad = -(-(d_inner + conv_dim + 128) // 768) * 768
    w_in = jnp.pad(ssm_in_proj[0], ((0, 0), (0, in_pad - in_dim))).astype(BF16)
    w_out = ssm_out_proj[0].astype(BF16)
    w_up = [ffn_up[i].astype(BF16) for i in range(depth)]
    w_down = [ffn_down[i].astype(BF16) for i in range(depth)]
    wk, wv, wq, wo = w_k.astype(BF16), w_v.astype(BF16), w_q[0].astype(BF16), w_o[0].astype(BF16)
    dtb = jnp.pad(ssm_dt_bias[0], (0, 128 - n_heads)).reshape(1, 128)
    alog = jnp.pad(ssm_a_log[0], (0, 128 - n_heads)).reshape(1, 128)
    dskip_hp = jnp.repeat(ssm_d[0], SSM_HEAD_DIM).reshape(1, d_inner)
    ssm_nw = ssm_norm[0].reshape(1, d_inner)
    expand = (lax.broadcasted_iota(jnp.int32, (128, d_inner), 0)
              == lax.broadcasted_iota(jnp.int32, (128, d_inner), 1) // SSM_HEAD_DIM).astype(BF16)
    cw0, cb0 = ssm_conv_w[0], ssm_conv_b[0].reshape(1, conv_dim)
    lambda_init = 0.8 - 0.6 * math.exp(-0.3 * 1)
    lvec = jnp.concatenate([lambda_q1, lambda_k1, lambda_q2, lambda_k2], axis=0)
    sub_w = subln[0].reshape(1, hw)
    scale = (hw // 2) ** -0.5

    g0 = norm_mix[0:1]
    zx_p = norm_matmul_resident(xp, g0, w_in, tm=512, tn=768, name="in_proj_p")
    gact_p, tail_p, ssm_state_p = ssd_prompt(zx_p, bp, seq, d_inner, cw0, cb0, dtb, alog, dskip_hp,
                                            ssm_nw, expand)
    xp = matmul_residual(gact_p, w_out, xp, name="out_proj_p")
    ssm_conv_p = tail_p[:, 8 - (SSM_CONV_W - 1):, :]

    (zx_s,) = norm_matmul(xs, g0, [0], [w_in], [[F32]], tn=768, name="in_proj_s")
    gact_s, newbuf_s, ssm_state_s = ssm_step_sample(
        zx_s, jnp.swapaxes(state_conv_ssm[0], 0, 1), state_ssm[0], d_inner, cw0, cb0, dtb, alog,
        dskip_hp, ssm_nw, expand)
    xs = matmul_residual(gact_s, w_out, xs, name="out_proj_s")
    ssm_conv_s = jnp.swapaxes(newbuf_s, 0, 1)

    ffn_conv_p, ffn_conv_s = [], []

    def conv_ffn(i, xp, xs, final_gain):
        gain = norm_ffn[i:i + 1]
        cw, cb = ffn_conv_w[i], ffn_conv_b[i].reshape(1, f)
        tm = _row_tile(seq, 512)
        h_p, tail = ffn_up_prompt(xp, gain, w_up[i], cw, cb, seq, tm=tm, tn=f // 2, name=f"ffn_up_p{i}")
        xp = matmul_residual(h_p, w_down[i], xp, final_gain, name=f"ffn_down_p{i}")
        tiles_per_seq = seq // tm
        tail = tail.reshape(bp, tiles_per_seq, 8, f)[:, -1, 8 - (FFN_CONV_W - 1):, :]
        ffn_conv_p.append(tail)
        (gu,) = norm_matmul(xs, gain, [0], [w_up[i]], [[F32]], tn=f, name=f"ffn_up_s{i}")
        h_s, nb = ffn_gate_sample(gu, jnp.swapaxes(state_conv_ffn[i], 0, 1), cw, cb)
        xs = matmul_residual(h_s, w_down[i], xs, final_gain, name=f"ffn_down_s{i}")
        ffn_conv_s.append(jnp.swapaxes(nb, 0, 1))
        return xp, xs

    xp, xs = conv_ffn(0, xp, xs, None)

    gains = jnp.concatenate([norm_kv.reshape(1, d), norm_mix[1:2]], axis=0)
    kp, kp16, vp, vp16, qp16 = norm_matmul(
        xp, gains, [0, 0, 1], [wk, wv, wq], [[F32, BF16], [F32, BF16], [BF16]],
        scales=[1.0, 1.0, scale], tm=512, tn=attn_w, name="kvq_p")
    ks, vs, qs16 = norm_matmul(
        xs, gains, [0, 0, 1], [wk, wv, wq], [[F32], [F32], [BF16]],
        scales=[1.0, 1.0, scale], tn=attn_w, name="kvq_s")

    o_p = diff_attention_prompt(qp16.reshape(bp, seq, attn_w), kp16.reshape(bp, seq, attn_w),
                                vp16.reshape(bp, seq, attn_w), lvec, sub_w, lambda_init)
    xp = matmul_residual(o_p.reshape(mp, attn_w), wo, xp, name="attn_out_p")
    o_s = diff_attention_sample(qs16.reshape(bd, ATTN_HEADS, hw), ks.reshape(bd, ATTN_HEADS, hw),
                                vs.reshape(bd, ATTN_HEADS, hw), cache_k, cache_v, page_table,
                                lvec, sub_w, lambda_init)
    xs = matmul_residual(o_s.reshape(bd, attn_w), wo, xs, name="attn_out_s")

    xp, xs = conv_ffn(1, xp, xs, norm_final.reshape(1, d))

    return (xp.reshape(bp, seq, d), xs.reshape(bd, 1, d),
            ssm_state_p[None], ssm_state_s[None], ssm_conv_p[None], ssm_conv_s[None],
            jnp.stack(ffn_conv_p), jnp.stack(ffn_conv_s),
            kp.reshape(bp, seq, ATTN_HEADS, hw), vp.reshape(bp, seq, ATTN_HEADS, hw),
            ks.reshape(bd, 1, ATTN_HEADS, hw), vs.reshape(bd, 1, ATTN_HEADS, hw))
```

```python
import functools
import math

import jax
import jax.numpy as jnp
from jax import lax
from jax.experimental import pallas as pl
from jax.experimental.pallas import tpu as pltpu

F32 = jnp.float32
BF16 = jnp.bfloat16
EPS = 1e-5
LOG2E = 1.4426950408889634

SSM_HEAD_DIM = 64
SSM_GROUPS = 4
SSM_D_STATE = 128
SSM_CONV_W = 4
SSD_CHUNK = 128
ATTN_HEADS = 8
FFN_CONV_W = 3
PAGES_PER_STEP = 8
DEC_GROUP_PAGES = 2
DEC_SCORE_CHUNK = 16
DEC_PROB_CHUNK = 8

VMEM_LIMIT_BYTES = 56 * 1024 * 1024


def _cparams(*sem):
    return pltpu.CompilerParams(dimension_semantics=sem, vmem_limit_bytes=VMEM_LIMIT_BYTES)


def _silu(x):
    h = 0.5 * x
    return h + h * jnp.tanh(h)


def _rms_unit(x):
    return x * lax.rsqrt(jnp.mean(x * x, axis=-1, keepdims=True) + EPS)


def _split3(a):
    hi = a.astype(BF16)
    r = a - hi.astype(F32)
    mid = r.astype(BF16)
    lo = (r - mid.astype(F32)).astype(BF16)
    return hi, mid, lo


def _dot(a, b):
    return jnp.dot(a, b, preferred_element_type=F32)


def _dot_nt(a, b):
    return lax.dot_general(a, b, (((1,), (1,)), ((), ())), preferred_element_type=F32)


def _dot_exact_rhs(a, b_bf16):
    hi, mid, lo = _split3(a)
    return _dot(hi, b_bf16) + _dot(mid, b_bf16) + _dot(lo, b_bf16)


def _dot_exact_lhs(a_bf16, b):
    hi, mid, lo = _split3(b)
    return _dot(a_bf16, hi) + _dot(a_bf16, mid) + _dot(a_bf16, lo)


def _row_tile(m, pref):
    t = min(m, pref)
    assert m % t == 0, (m, t)
    return t


def _norm_mm_kernel(n_w, gain_idx, scales, out_dtypes, x_ref, g_ref, *rest):
    w_refs = rest[:n_w]
    o_refs = rest[n_w:-1]
    xn_ref = rest[-1]

    @pl.when(pl.program_id(1) == 0)
    def _():
        xh = _rms_unit(x_ref[...])
        for i in range(n_w):
            gi = gain_idx[i]
            xn_ref[i] = (xh * g_ref[gi:gi + 1, :]).astype(BF16)

    k = 0
    for i in range(n_w):
        r = _dot(xn_ref[i], w_refs[i][...])
        if scales[i] != 1.0:
            r = r * scales[i]
        for dt in out_dtypes[i]:
            o_refs[k][...] = r.astype(dt)
            k += 1


def norm_matmul(x, gains, gain_idx, weights, out_dtypes, scales=None, tm=512, tn=None, name="norm_mm"):
    m, kdim = x.shape
    n = weights[0].shape[1]
    n_w = len(weights)
    scales = tuple(scales) if scales is not None else (1.0,) * n_w
    tm = _row_tile(m, tm)
    tn = n if tn is None else tn
    assert n % tn == 0
    outs, specs = [], []
    for dts in out_dtypes:
        for dt in dts:
            outs.append(jax.ShapeDtypeStruct((m, n), dt))
            specs.append(pl.BlockSpec((tm, tn), lambda i, j: (i, j)))
    kern = functools.partial(_norm_mm_kernel, n_w, tuple(gain_idx), scales,
                             tuple(tuple(d) for d in out_dtypes))
    return pl.pallas_call(
        kern,
        grid=(m // tm, n // tn),
        in_specs=[pl.BlockSpec((tm, kdim), lambda i, j: (i, 0)),
                  pl.BlockSpec(gains.shape, lambda i, j: (0, 0))]
        + [pl.BlockSpec((kdim, tn), lambda i, j: (0, j)) for _ in weights],
        out_specs=specs,
        out_shape=outs,
        scratch_shapes=[pltpu.VMEM((n_w, tm, kdim), BF16)],
        compiler_params=_cparams("parallel", "arbitrary"),
        name=name,
    )(x, gains, *weights)


def _norm_mm_resident_kernel(tn, x_ref, g_ref, w_ref, o_ref):
    xn = (_rms_unit(x_ref[...]) * g_ref[...]).astype(BF16)
    for j in range(w_ref.shape[1] // tn):
        o_ref[:, j * tn:(j + 1) * tn] = _dot(xn, w_ref[:, j * tn:(j + 1) * tn])


def norm_matmul_resident(x, gain, w, tm=512, tn=768, name="norm_mm_res"):
    m, kdim = x.shape
    n = w.shape[1]
    tm = _row_tile(m, tm)
    assert n % tn == 0
    return pl.pallas_call(
        functools.partial(_norm_mm_resident_kernel, tn),
        grid=(m // tm,),
        in_specs=[pl.BlockSpec((tm, kdim), lambda i: (i, 0)),
                  pl.BlockSpec((1, kdim), lambda i: (0, 0)),
                  pl.BlockSpec((kdim, n), lambda i: (0, 0), pipeline_mode=pl.Buffered(1))],
        out_specs=pl.BlockSpec((tm, n), lambda i: (i, 0)),
        out_shape=jax.ShapeDtypeStruct((m, n), F32),
        compiler_params=_cparams("parallel"),
        name=name,
    )(x, gain, w)


def _mm_res_kernel(final_norm, a_ref, w_ref, r_ref, *rest):
    o_ref = rest[-1]
    y = r_ref[...] + _dot(a_ref[...], w_ref[...])
    if final_norm:
        y = _rms_unit(y) * rest[0][...]
    o_ref[...] = y


def matmul_residual(a, w, res, final_gain=None, tm=512, name="mm_res"):
    m, kdim = a.shape
    n = w.shape[1]
    tm = _row_tile(m, tm)
    ins = [a, w, res]
    specs = [pl.BlockSpec((tm, kdim), lambda i: (i, 0)),
             pl.BlockSpec((kdim, n), lambda i: (0, 0)),
             pl.BlockSpec((tm, n), lambda i: (i, 0))]
    if final_gain is not None:
        ins.append(final_gain)
        specs.append(pl.BlockSpec((1, n), lambda i: (0, 0)))
    return pl.pallas_call(
        functools.partial(_mm_res_kernel, final_gain is not None),
        grid=(m // tm,),
        in_specs=specs,
        out_specs=pl.BlockSpec((tm, n), lambda i: (i, 0)),
        out_shape=jax.ShapeDtypeStruct((m, n), F32),
        compiler_params=_cparams("parallel"),
        name=name,
    )(*ins)


HALO = 16


def _ffn_up_kernel(tm, seq, x_ref, halo_ref, g_ref, wg_ref, wu_ref, cw_ref, cb_ref,
                   h_ref, tail_ref, xn_ref):
    i = pl.program_id(0)

    @pl.when(pl.program_id(1) == 0)
    def _():
        gain = g_ref[...]
        xn_ref[HALO:, :] = (_rms_unit(x_ref[...]) * gain).astype(BF16)
        keep = jnp.where((i * tm) % seq == 0, 0.0, 1.0)
        xn_ref[0:HALO, :] = (_rms_unit(halo_ref[...]) * gain * keep).astype(BF16)

    g = _dot(xn_ref[...], wg_ref[...])
    u = _dot(xn_ref[HALO:, :], wu_ref[...])
    c = (cw_ref[2:3, :] * g + cw_ref[1:2, :] * pltpu.roll(g, 1, 0)
         + cw_ref[0:1, :] * pltpu.roll(g, 2, 0) + cb_ref[...])
    c = c[HALO:, :]
    h_ref[...] = (_silu(c) * u).astype(BF16)
    tail_ref[0] = g[tm + HALO - 8:, :]


def ffn_up_prompt(x, gain, w_up, cw, cb, seq, tm=512, tn=None, name="ffn_up"):
    m, d = x.shape
    f = cw.shape[1]
    tm = _row_tile(seq, tm)
    tn = f if tn is None else tn
    assert f % tn == 0 and tm % HALO == 0
    nj = f // tn
    hb = tm // HALO
    h, tail = pl.pallas_call(
        functools.partial(_ffn_up_kernel, tm, seq),
        grid=(m // tm, nj),
        in_specs=[pl.BlockSpec((tm, d), lambda i, j: (i, 0)),
                  pl.BlockSpec((HALO, d), lambda i, j: (jnp.maximum(i * hb - 1, 0), 0)),
                  pl.BlockSpec((1, d), lambda i, j: (0, 0)),
                  pl.BlockSpec((d, tn), lambda i, j: (0, j)),
                  pl.BlockSpec((d, tn), lambda i, j: (0, nj + j)),
                  pl.BlockSpec((FFN_CONV_W, tn), lambda i, j: (0, j)),
                  pl.BlockSpec((1, tn), lambda i, j: (0, j))],
        out_specs=[pl.BlockSpec((tm, tn), lambda i, j: (i, j)),
                   pl.BlockSpec((1, 8, tn), lambda i, j: (i, 0, j))],
        out_shape=[jax.ShapeDtypeStruct((m, f), BF16),
                   jax.ShapeDtypeStruct((m // tm, 8, f), F32)],
        scratch_shapes=[pltpu.VMEM((HALO + tm, d), BF16)],
        compiler_params=_cparams("parallel", "arbitrary"),
        name=name,
    )(x, x, gain, w_up, w_up, cw, cb)
    return h, tail


def _ssd_kernel(d_inner, n_heads, zx_ref, cw_ref, cb_ref, dtb_ref, alog_ref, dskip_ref, nw_ref,
                expand_ref, g_ref, tail_ref, state_ref, xpad_ref, ht_ref, y_ref):
    q = SSD_CHUNK
    ns = SSM_D_STATE
    gn = SSM_GROUPS * ns
    conv_dim = d_inner + 2 * gn
    gw = d_inner // SSM_GROUPS
    c = pl.program_id(1)
    last = pl.num_programs(1) - 1

    @pl.when(c == 0)
    def _():
        xpad_ref[0:8, :] = jnp.zeros((8, conv_dim), F32)
        ht_ref[...] = jnp.zeros(ht_ref.shape, F32)

    xpad_ref[8:, :] = zx_ref[:, d_inner:d_inner + conv_dim]
    conv = cb_ref[...] + cw_ref[SSM_CONV_W - 1:SSM_CONV_W, :] * xpad_ref[8:, :]
    for k in range(1, SSM_CONV_W):
        conv = conv + cw_ref[SSM_CONV_W - 1 - k:SSM_CONV_W - k, :] * xpad_ref[pl.ds(8 - k, q), :]
    new_tail = xpad_ref[q:q + 8, :]
    xpad_ref[0:8, :] = new_tail
    xbc = _silu(conv)
    xs = xbc[:, :d_inner]
    bm = xbc[:, d_inner:d_inner + gn]
    cm = xbc[:, d_inner + gn:]

    dt_raw = zx_ref[:, d_inner + conv_dim:d_inner + conv_dim + 128] + dtb_ref[...]
    dt = jnp.maximum(dt_raw, 0.0) + jnp.log1p(jnp.exp(-jnp.abs(dt_raw)))
    da = dt * (-jnp.exp(alog_ref[...]))
    row = lax.broadcasted_iota(jnp.int32, (q, q), 0)
    col = lax.broadcasted_iota(jnp.int32, (q, q), 1)
    tri = row >= col
    a_cum = _dot_exact_lhs(jnp.where(tri, 1.0, 0.0).astype(BF16), da)
    a_cum_t = jnp.transpose(a_cum)

    expand = expand_ref[...]
    dt_hp = _dot_exact_rhs(dt, expand)
    a_hp = _dot_exact_rhs(a_cum, expand)
    xdt = xs * dt_hp
    a_last = a_hp[q - 1:q, :]
    xdt_b = xdt.astype(BF16)
    dec_x = (jnp.exp(a_last - a_hp) * xdt).astype(BF16)
    off_scale = jnp.exp(a_hp)
    chunk_decay = jnp.exp(a_last)

    lane = lax.broadcasted_iota(jnp.int32, (q, 2 * SSM_HEAD_DIM), 1)
    heads_per_group = n_heads // SSM_GROUPS
    for g in range(SSM_GROUPS):
        bg = bm[:, g * ns:(g + 1) * ns]
        cg = cm[:, g * ns:(g + 1) * ns].astype(BF16)
        cb_mat = _dot_nt(cg, bg.astype(BF16))
        cols = slice(g * gw, (g + 1) * gw)
        ht = ht_ref[g]
        y_off = _dot(cg, ht.astype(BF16)) * off_scale[:, cols]
        new_states = _dot(jnp.transpose(bg).astype(BF16), dec_x[:, cols])
        ht_ref[g] = ht * chunk_decay[:, cols] + new_states
        for pr in range(heads_per_group // 2):
            c0 = g * gw + pr * 2 * SSM_HEAD_DIM
            x2 = xdt_b[:, c0:c0 + 2 * SSM_HEAD_DIM]
            yd = None
            for half in range(2):
                hh = g * heads_per_group + pr * 2 + half
                seg = a_cum[:, hh:hh + 1] - a_cum_t[hh:hh + 1, :]
                lmat = jnp.where(tri, jnp.exp(jnp.minimum(seg, 0.0)), 0.0)
                mm = (cb_mat * lmat).astype(BF16)
                xh = jnp.where((lane >= SSM_HEAD_DIM) == (half == 1), x2, jnp.zeros_like(x2))
                t = _dot(mm, xh)
                yd = t if yd is None else yd + t
            y_ref[:, c0:c0 + 2 * SSM_HEAD_DIM] = yd + y_off[:, pr * 2 * SSM_HEAD_DIM:(pr + 1) * 2 * SSM_HEAD_DIM]

    y = y_ref[...] + dskip_ref[...] * xs
    gated = y * _silu(zx_ref[:, 0:d_inner])
    for g in range(SSM_GROUPS):
        cols = slice(g * gw, (g + 1) * gw)
        g_ref[:, cols] = (_rms_unit(gated[:, cols]) * nw_ref[:, cols]).astype(BF16)

    @pl.when(c == last)
    def _():
        tail_ref[0] = new_tail
        for g in range(SSM_GROUPS):
            st = jnp.transpose(ht_ref[g])
            state_ref[0, g * heads_per_group:(g + 1) * heads_per_group] = st.reshape(
                heads_per_group, SSM_HEAD_DIM, ns)


def ssd_prompt(zx, batch, seq, d_inner, cw, cb, dtb, alog, dskip_hp, nw, expand, name="ssd"):
    m, width = zx.shape
    n_heads = d_inner // SSM_HEAD_DIM
    conv_dim = cw.shape[1]
    nc = seq // SSD_CHUNK
    q = SSD_CHUNK
    full = lambda shape: pl.BlockSpec(shape, lambda b, c: (0,) * len(shape))
    return pl.pallas_call(
        functools.partial(_ssd_kernel, d_inner, n_heads),
        grid=(batch, nc),
        in_specs=[pl.BlockSpec((q, width), lambda b, c: (b * nc + c, 0)),
                  full(cw.shape), full(cb.shape), full(dtb.shape), full(alog.shape),
                  full(dskip_hp.shape), full(nw.shape), full(expand.shape)],
        out_specs=[pl.BlockSpec((q, d_inner), lambda b, c: (b * nc + c, 0)),
                   pl.BlockSpec((1, 8, conv_dim), lambda b, c: (b, 0, 0)),
                   pl.BlockSpec((1, n_heads, SSM_HEAD_DIM, SSM_D_STATE), lambda b, c: (b, 0, 0, 0))],
        out_shape=[jax.ShapeDtypeStruct((m, d_inner), BF16),
                   jax.ShapeDtypeStruct((batch, 8, conv_dim), F32),
                   jax.ShapeDtypeStruct((batch, n_heads, SSM_HEAD_DIM, SSM_D_STATE), F32)],
        scratch_shapes=[pltpu.VMEM((q + 8, conv_dim), F32),
                        pltpu.VMEM((SSM_GROUPS, SSM_D_STATE, d_inner // SSM_GROUPS), F32),
                        pltpu.VMEM((q, d_inner), F32)],
        compiler_params=_cparams("parallel", "arbitrary"),
        name=name,
    )(zx, cw, cb, dtb, alog, dskip_hp, nw, expand)


def _diff_lambda(lv, lambda_init):
    s1 = jnp.sum(lv[0:1, :] * lv[1:2, :], axis=-1, keepdims=True)
    s2 = jnp.sum(lv[2:3, :] * lv[3:4, :], axis=-1, keepdims=True)
    return jnp.exp(s1) - jnp.exp(s2) + lambda_init


ATTN_ROW_CHUNK = 64
ATTN_HEADS_PER_STEP = 2


def _attn_kernel(tq, n_hps, lambda_init, q_ref, k_ref, v_ref, lv_ref, sub_ref, o_ref, *scratch):
    qi = pl.program_id(2)
    hw = q_ref.shape[2] // n_hps
    hd = hw // 2
    rc = ATTN_ROW_CHUNK
    per_head = len(scratch) // n_hps
    heads = [scratch[h * per_head:(h + 1) * per_head] for h in range(n_hps)]
    ones = jnp.ones((tq, hw), BF16)

    for h, (q2_ref, _, _, _, m_ref, _, acc_ref) in enumerate(heads):
        qh = q_ref[0, :, h * hw:(h + 1) * hw]
        lane = lax.broadcasted_iota(jnp.int32, qh.shape, 1)
        zero = jnp.zeros_like(qh)
        q2_ref[0:tq, :] = jnp.where(lane < hd, qh, zero)
        q2_ref[tq:, :] = jnp.where(lane >= hd, qh, zero)
        m_ref[...] = jnp.full(m_ref.shape, -jnp.inf, F32)
        acc_ref[...] = jnp.zeros(acc_ref.shape, F32)

    def qk(h, kb, odd):
        q2_ref = heads[h][0]
        s_ref = heads[h][2 if odd else 1]
        start = pl.multiple_of(kb * tq, tq)
        s_ref[...] = _dot_nt(q2_ref[...], k_ref[0, pl.ds(start, tq), h * hw:(h + 1) * hw])

    def softmax_pv(h, kb, odd, masked):
        _, s0_ref, s1_ref, p_ref, m_ref, c_ref, acc_ref = heads[h]
        s_ref = s1_ref if odd else s0_ref
        start = pl.multiple_of(kb * tq, tq)
        for r0 in range(0, 2 * tq, rc):
            rows = slice(r0, r0 + rc)
            s = s_ref[rows, :]
            if masked:
                r = lax.broadcasted_iota(jnp.int32, (rc, tq), 0) + (r0 % tq)
                cidx = lax.broadcasted_iota(jnp.int32, (rc, tq), 1)
                s = jnp.where(r >= cidx, s, -jnp.inf)
            m_old = m_ref[rows, :]
            m_new = jnp.maximum(m_old, jnp.max(s, axis=-1, keepdims=True))
            m_ref[rows, :] = m_new
            c_ref[rows, :] = jnp.exp(m_old - m_new)
            p_ref[rows, :] = jnp.exp(s - m_new).astype(BF16)
        v_ext = jnp.concatenate([v_ref[0, pl.ds(start, tq), h * hw:(h + 1) * hw], ones], axis=1)
        pv = _dot(p_ref[...], v_ext)
        for r0 in range(0, 2 * tq, rc):
            rows = slice(r0, r0 + rc)
            acc_ref[rows, :] = acc_ref[rows, :] * c_ref[rows, :] + pv[rows, :]

    def each_head(fn, *args):
        for h in range(n_hps):
            fn(h, *args)

    each_head(qk, 0, False)

    def body(kk, carry):
        kb = 2 * kk
        each_head(qk, kb + 1, True)
        each_head(softmax_pv, kb, False, False)
        each_head(qk, kb + 2, False)
        each_head(softmax_pv, kb + 1, True, False)
        return carry

    lax.fori_loop(0, qi // 2, body, 0)

    @pl.when(qi % 2 == 0)
    def _():
        each_head(softmax_pv, qi, False, True)

    @pl.when(qi % 2 == 1)
    def _():
        each_head(qk, qi, True)
        each_head(softmax_pv, qi - 1, False, False)
        each_head(softmax_pv, qi, True, True)

    lam = _diff_lambda(lv_ref[...], lambda_init)
    for h in range(n_hps):
        acc = heads[h][6][...]
        o = acc[:, :hw] / acc[:, hw:]
        o = o[:tq] - lam * o[tq:]
        o = _rms_unit(o) * sub_ref[...]
        o_ref[0, :, h * hw:(h + 1) * hw] = (o * (1.0 - lambda_init)).astype(BF16)


def diff_attention_prompt(q, k, v, lvec, subln, lambda_init, tq=512, name="diff_attn"):
    b, l, width = q.shape
    hw = width // ATTN_HEADS
    n_hps = ATTN_HEADS_PER_STEP if ATTN_HEADS % ATTN_HEADS_PER_STEP == 0 else 1
    bw = n_hps * hw
    tq = _row_tile(l, tq)
    assert tq % ATTN_ROW_CHUNK == 0
    head_scratch = [pltpu.VMEM((2 * tq, hw), BF16),
                    pltpu.VMEM((2 * tq, tq), F32),
                    pltpu.VMEM((2 * tq, tq), F32),
                    pltpu.VMEM((2 * tq, tq), BF16),
                    pltpu.VMEM((2 * tq, 1), F32),
                    pltpu.VMEM((2 * tq, 1), F32),
                    pltpu.VMEM((2 * tq, 2 * hw), F32)]
    return pl.pallas_call(
        functools.partial(_attn_kernel, tq, n_hps, lambda_init),
        grid=(b, ATTN_HEADS // n_hps, l // tq),
        in_specs=[pl.BlockSpec((1, tq, bw), lambda bi, h, i: (bi, i, h)),
                  pl.BlockSpec((1, l, bw), lambda bi, h, i: (bi, 0, h)),
                  pl.BlockSpec((1, l, bw), lambda bi, h, i: (bi, 0, h)),
                  pl.BlockSpec(lvec.shape, lambda bi, h, i: (0, 0)),
                  pl.BlockSpec(subln.shape, lambda bi, h, i: (0, 0))],
        out_specs=pl.BlockSpec((1, tq, bw), lambda bi, h, i: (bi, i, h)),
        out_shape=jax.ShapeDtypeStruct((b, l, width), BF16),
        scratch_shapes=head_scratch * n_hps,
        compiler_params=_cparams("parallel", "parallel", "arbitrary"),
        name=name,
    )(q, k, v, lvec, subln)


def _ssm_prep_kernel(d_inner, zx_ref, buf_ref, cw_ref, cb_ref, dtb_ref, alog_ref, expand_ref,
                     newbuf_ref, xs_ref, xdt_ref, b_ref, c_ref, decay_ref):
    gn = SSM_GROUPS * SSM_D_STATE
    conv_dim = d_inner + 2 * gn
    xnew = zx_ref[:, d_inner:d_inner + conv_dim]
    conv = cb_ref[...] + cw_ref[SSM_CONV_W - 1:SSM_CONV_W, :] * xnew
    for k in range(SSM_CONV_W - 1):
        conv = conv + cw_ref[k:k + 1, :] * buf_ref[k]
    for k in range(SSM_CONV_W - 2):
        newbuf_ref[k] = buf_ref[k + 1]
    newbuf_ref[SSM_CONV_W - 2] = xnew
    xbc = _silu(conv)
    xs = xbc[:, :d_inner]
    dt_raw = zx_ref[:, d_inner + conv_dim:d_inner + conv_dim + 128] + dtb_ref[...]
    dt = jnp.maximum(dt_raw, 0.0) + jnp.log1p(jnp.exp(-jnp.abs(dt_raw)))
    decay_ref[...] = jnp.exp(dt * (-jnp.exp(alog_ref[...])))
    xs_ref[...] = xs
    xdt_ref[...] = xs * _dot_exact_rhs(dt, expand_ref[...])
    b_ref[...] = xbc[:, d_inner:d_inner + gn]
    c_ref[...] = xbc[:, d_inner + gn:]


def _col_bcast(rowvec):
    n = rowvec.shape[1]
    wide = jnp.broadcast_to(rowvec, (128, n))
    return jnp.concatenate([jnp.transpose(wide[:, j * 128:(j + 1) * 128]) for j in range(n // 128)], axis=0)


def _ssm_state_kernel(n_heads, h0_ref, xdt_ref, decay_ref, b_ref, c_ref, hn_ref, y_ref):
    ns = SSM_D_STATE
    hpg = n_heads // SSM_GROUPS
    xcol = _col_bcast(xdt_ref[0])
    dcol = _col_bcast(decay_ref[0])
    for g in range(SSM_GROUPS):
        brow = b_ref[0, :, g * ns:(g + 1) * ns]
        crow = jnp.broadcast_to(c_ref[0, :, g * ns:(g + 1) * ns], (8, ns))
        chi, cmid, _ = _split3(crow)
        parts = []
        for r in range(hpg):
            h = g * hpg + r
            hn = (h0_ref[0, h] * dcol[h:h + 1, :]
                  + xcol[h * SSM_HEAD_DIM:(h + 1) * SSM_HEAD_DIM, :] * brow)
            hn_ref[0, h] = hn
            parts.append(hn)
        hhi, hmid, _ = _split3(jnp.concatenate(parts, axis=0))
        yr = _dot_nt(chi, hhi) + _dot_nt(cmid, hhi) + _dot_nt(chi, hmid)
        y_ref[0, :, g * hpg * SSM_HEAD_DIM:(g + 1) * hpg * SSM_HEAD_DIM] = yr[0:1, :]


def _ssm_post_kernel(d_inner, y_ref, xs_ref, zx_ref, dskip_ref, nw_ref, g_ref):
    gw = d_inner // SSM_GROUPS
    y = y_ref[...] + dskip_ref[...] * xs_ref[...]
    gated = y * _silu(zx_ref[:, 0:d_inner])
    for g in range(SSM_GROUPS):
        cols = slice(g * gw, (g + 1) * gw)
        g_ref[:, cols] = (_rms_unit(gated[:, cols]) * nw_ref[:, cols]).astype(BF16)


def ssm_step_sample(zx, conv_buf_t, h0, d_inner, cw, cb, dtb, alog, dskip_hp, nw, expand):
    bsz = zx.shape[0]
    n_heads = d_inner // SSM_HEAD_DIM
    gn = SSM_GROUPS * SSM_D_STATE
    conv_dim = d_inner + 2 * gn
    newbuf, xs, xdt, bm, cm, decay = pl.pallas_call(
        functools.partial(_ssm_prep_kernel, d_inner),
        out_shape=[jax.ShapeDtypeStruct(conv_buf_t.shape, F32),
                   jax.ShapeDtypeStruct((bsz, d_inner), F32),
                   jax.ShapeDtypeStruct((bsz, d_inner), F32),
                   jax.ShapeDtypeStruct((bsz, gn), F32),
                   jax.ShapeDtypeStruct((bsz, gn), F32),
                   jax.ShapeDtypeStruct((bsz, 128), F32)],
        compiler_params=pltpu.CompilerParams(vmem_limit_bytes=VMEM_LIMIT_BYTES),
        name="ssm_prep",
    )(zx, conv_buf_t, cw, cb, dtb, alog, expand)
    row3 = lambda a: a.reshape(bsz, 1, a.shape[1])
    hn, y = pl.pallas_call(
        functools.partial(_ssm_state_kernel, n_heads),
        grid=(bsz,),
        in_specs=[pl.BlockSpec((1, n_heads, SSM_HEAD_DIM, SSM_D_STATE), lambda b: (b, 0, 0, 0)),
                  pl.BlockSpec((1, 1, d_inner), lambda b: (b, 0, 0)),
                  pl.BlockSpec((1, 1, 128), lambda b: (b, 0, 0)),
                  pl.BlockSpec((1, 1, gn), lambda b: (b, 0, 0)),
                  pl.BlockSpec((1, 1, gn), lambda b: (b, 0, 0))],
        out_specs=[pl.BlockSpec((1, n_heads, SSM_HEAD_DIM, SSM_D_STATE), lambda b: (b, 0, 0, 0)),
                   pl.BlockSpec((1, 1, d_inner), lambda b: (b, 0, 0))],
        out_shape=[jax.ShapeDtypeStruct(h0.shape, F32),
                   jax.ShapeDtypeStruct((bsz, 1, d_inner), F32)],
        compiler_params=_cparams("parallel"),
        name="ssm_state",
    )(h0, row3(xdt), row3(decay), row3(bm), row3(cm))
    gact = pl.pallas_call(
        functools.partial(_ssm_post_kernel, d_inner),
        out_shape=jax.ShapeDtypeStruct((bsz, d_inner), BF16),
        compiler_params=pltpu.CompilerParams(vmem_limit_bytes=VMEM_LIMIT_BYTES),
        name="ssm_post",
    )(y.reshape(bsz, d_inner), xs, zx, dskip_hp, nw)
    return gact, newbuf, hn


def _ffn_gate_sample_kernel(f, gu_ref, buf_ref, cw_ref, cb_ref, h_ref, newbuf_ref):
    g = gu_ref[:, 0:f]
    u = gu_ref[:, f:2 * f]
    c = cb_ref[...] + cw_ref[FFN_CONV_W - 1:FFN_CONV_W, :] * g
    for k in range(FFN_CONV_W - 1):
        c = c + cw_ref[k:k + 1, :] * buf_ref[k]
    for k in range(FFN_CONV_W - 2):
        newbuf_ref[k] = buf_ref[k + 1]
    newbuf_ref[FFN_CONV_W - 2] = g
    h_ref[...] = (_silu(c) * u).astype(BF16)


def ffn_gate_sample(gu, buf_t, cw, cb):
    bsz = gu.shape[0]
    f = cw.shape[1]
    return pl.pallas_call(
        functools.partial(_ffn_gate_sample_kernel, f),
        out_shape=[jax.ShapeDtypeStruct((bsz, f), BF16),
                   jax.ShapeDtypeStruct(buf_t.shape, F32)],
        compiler_params=pltpu.CompilerParams(vmem_limit_bytes=VMEM_LIMIT_BYTES),
        name="ffn_gate_sample",
    )(gu, buf_t, cw, cb)


def _dec_attn_kernel(npg, lambda_init, pt_ref, q_ref, kn_ref, vn_ref, lv_ref, sub_ref, *rest):
    k_refs = rest[:npg]
    v_refs = rest[npg:2 * npg]
    o_ref = rest[2 * npg]
    s_ref, m_ref, l_ref, acc_ref = rest[2 * npg + 1:]
    del pt_ref
    j = pl.program_id(1)
    page, nh, hw = k_refs[0].shape[1:]
    hd = hw // 2
    sc, pc = DEC_SCORE_CHUNK, DEC_PROB_CHUNK

    qv = q_ref[0].astype(F32) * LOG2E
    srow = lax.broadcasted_iota(jnp.int32, (hw, hw), 0)
    scol = lax.broadcasted_iota(jnp.int32, (hw, hw), 1)
    seg = jnp.where(srow // hd == scol // hd, 1.0, 0.0).astype(BF16)

    def scores(k_tok):
        n_tok = k_tok.shape[0]
        prod = (k_tok * qv[None]).reshape(n_tok * nh, hw).astype(BF16)
        return _dot(prod, seg).reshape(n_tok, nh, hw)

    def swap_halves(x):
        return pltpu.roll(x.reshape(-1, hw), hd, 1).reshape(x.shape)

    @pl.when(j == 0)
    def _():
        m_ref[...] = scores(kn_ref[...])[0]
        l_ref[...] = jnp.ones(l_ref.shape, F32)
        vn = vn_ref[0]
        acc_ref[...] = jnp.concatenate([vn, vn], axis=1)

    gp = DEC_GROUP_PAGES
    n_groups = npg // gp

    def score_chunks(g):
        return [(p, t0) for p in range(g * gp, (g + 1) * gp) for t0 in range(0, page, sc)]

    def prob_chunks(g):
        return [(p, t0) for p in range(g * gp, (g + 1) * gp) for t0 in range(0, page, pc)]

    def score_chunk(p, t0, m_run):
        s = scores(k_refs[p][0, t0:t0 + sc])
        s_ref[p, t0:t0 + sc] = s
        return jnp.maximum(m_run, jnp.max(s, axis=0))

    m_cur = m_ref[...]
    l_run = l_ref[...]
    acc = acc_ref[...]
    acc_a, acc_b = acc[:, :hw], acc[:, hw:]
    m_next = m_cur
    for p, t0 in score_chunks(0):
        m_next = score_chunk(p, t0, m_next)
    for g in range(n_groups):
        m_new = m_next
        corr = jnp.exp2(m_cur - m_new)
        l_run = l_run * corr
        acc_a = acc_a * corr
        acc_b = acc_b * swap_halves(corr)
        m_cur = m_new
        pending = score_chunks(g + 1) if g + 1 < n_groups else []
        every = sc // pc
        for i, (p, t0) in enumerate(prob_chunks(g)):
            if pending and i % every == 0:
                m_next = score_chunk(*pending.pop(0), m_next)
            pr = jnp.exp2(s_ref[p, t0:t0 + pc] - m_new[None])
            l_run = l_run + jnp.sum(pr, axis=0)
            vp = v_refs[p][0, t0:t0 + pc]
            acc_a = acc_a + jnp.sum(pr * vp, axis=0)
            acc_b = acc_b + jnp.sum(swap_halves(pr) * vp, axis=0)
        assert not pending
    m_ref[...] = m_cur
    l_ref[...] = l_run
    acc_ref[...] = jnp.concatenate([acc_a, acc_b], axis=1)

    @pl.when(j == pl.num_programs(1) - 1)
    def _():
        xa = acc_a / l_run
        xb = acc_b / swap_halves(l_run)
        lam = _diff_lambda(lv_ref[...], lambda_init)
        low = lax.broadcasted_iota(jnp.int32, (nh, hw), 1) < hd
        o = jnp.where(low, xa - lam * xb, xb - lam * xa)
        o = _rms_unit(o) * sub_ref[...]
        o_ref[0] = (o * (1.0 - lambda_init)).astype(BF16)


def diff_attention_sample(q, k_new, v_new, cache_k, cache_v, page_table, lvec, subln, lambda_init):
    bsz, nh, hw = q.shape
    n_pages = page_table.shape[1]
    page = cache_k.shape[1]
    npg = min(PAGES_PER_STEP, n_pages)
    assert n_pages % npg == 0
    row = pl.BlockSpec((1, nh, hw), lambda b, j, pt: (b, 0, 0))

    def page_spec(p):
        return pl.BlockSpec((1, page, nh, hw), lambda b, j, pt: (pt[b, j * npg + p], 0, 0, 0))

    grid_spec = pltpu.PrefetchScalarGridSpec(
        num_scalar_prefetch=1,
        grid=(bsz, n_pages // npg),
        in_specs=[row, row, row,
                  pl.BlockSpec(lvec.shape, lambda b, j, pt: (0, 0)),
                  pl.BlockSpec(subln.shape, lambda b, j, pt: (0, 0))]
        + [page_spec(p) for p in range(npg)] + [page_spec(p) for p in range(npg)],
        out_specs=row,
        scratch_shapes=[pltpu.VMEM((npg, page, nh, hw), F32),
                        pltpu.VMEM((nh, hw), F32),
                        pltpu.VMEM((nh, hw), F32),
                        pltpu.VMEM((nh, 2 * hw), F32)],
    )
    return pl.pallas_call(
        functools.partial(_dec_attn_kernel, npg, lambda_init),
        grid_spec=grid_spec,
        out_shape=jax.ShapeDtypeStruct((bsz, nh, hw), BF16),
        compiler_params=_cparams("parallel", "arbitrary"),
        name="diff_attn_sample",
    )(page_table, q, k_new, v_new, lvec, subln, *([cache_k] * npg), *([cache_v] * npg))


def kernel(x_prompt, x_sample, state_ssm, state_conv_ssm, state_conv_ffn, cache_k, cache_v, page_table,
           norm_mix, norm_ffn, ssm_in_proj, ssm_conv_w, ssm_conv_b, ssm_dt_bias, ssm_a_log, ssm_d,
           ssm_norm, ssm_out_proj, ffn_up, ffn_conv_w, ffn_conv_b, ffn_down, norm_kv, w_k, w_v,
           w_q, lambda_q1, lambda_k1, lambda_q2, lambda_k2, subln, w_o, norm_final):
    bp, seq, d = x_prompt.shape
    bd = x_sample.shape[0]
    depth = norm_mix.shape[0]
    assert depth == 2 and ssm_in_proj.shape[0] == 1 and x_sample.shape[1] == 1
    d_inner = ssm_norm.shape[1]
    n_heads = ssm_dt_bias.shape[1]
    assert n_heads * SSM_HEAD_DIM == d_inner and n_heads <= 128
    conv_dim = ssm_conv_w.shape[2]
    f = ffn_conv_w.shape[2]
    attn_w = w_k.shape[1]
    hw = attn_w // ATTN_HEADS
    mp = bp * seq

    xp = x_prompt.reshape(mp, d)
    xs = x_sample.reshape(bd, d)

    in_dim = ssm_in_proj.shape[2]
    in_pad = -(-(d_inner + conv_dim + 128) // 768) * 768
    w_in = jnp.pad(ssm_in_proj[0], ((0, 0), (0, in_pad - in_dim))).astype(BF16)
    w_out = ssm_out_proj[0].astype(BF16)
    w_up = [ffn_up[i].astype(BF16) for i in range(depth)]
    w_down = [ffn_down[i].astype(BF16) for i in range(depth)]
    wk, wv, wq, wo = w_k.astype(BF16), w_v.astype(BF16), w_q[0].astype(BF16), w_o[0].astype(BF16)
    dtb = jnp.pad(ssm_dt_bias[0], (0, 128 - n_heads)).reshape(1, 128)
    alog = jnp.pad(ssm_a_log[0], (0, 128 - n_heads)).reshape(1, 128)
    dskip_hp = jnp.repeat(ssm_d[0], SSM_HEAD_DIM).reshape(1, d_inner)
    ssm_nw = ssm_norm[0].reshape(1, d_inner)
    expand = (lax.broadcasted_iota(jnp.int32, (128, d_inner), 0)
              == lax.broadcasted_iota(jnp.int32, (128, d_inner), 1) // SSM_HEAD_DIM).astype(BF16)
    cw0, cb0 = ssm_conv_w[0], ssm_conv_b[0].reshape(1, conv_dim)
    lambda_init = 0.8 - 0.6 * math.exp(-0.3 * 1)
    lvec = jnp.concatenate([lambda_q1, lambda_k1, lambda_q2, lambda_k2], axis=0)
    sub_w = subln[0].reshape(1, hw)
    scale = (hw // 2) ** -0.5

    g0 = norm_mix[0:1]
    zx_p = norm_matmul_resident(xp, g0, w_in, tm=512, tn=768, name="in_proj_p")
    gact_p, tail_p, ssm_state_p = ssd_prompt(zx_p, bp, seq, d_inner, cw0, cb0, dtb, alog, dskip_hp,
                                            ssm_nw, expand)
    xp = matmul_residual(gact_p, w_out, xp, name="out_proj_p")
    ssm_conv_p = tail_p[:, 8 - (SSM_CONV_W - 1):, :]

    (zx_s,) = norm_matmul(xs, g0, [0], [w_in], [[F32]], tn=768, name="in_proj_s")
    gact_s, newbuf_s, ssm_state_s = ssm_step_sample(
        zx_s, jnp.swapaxes(state_conv_ssm[0], 0, 1), state_ssm[0], d_inner, cw0, cb0, dtb, alog,
        dskip_hp, ssm_nw, expand)
    xs = matmul_residual(gact_s, w_out, xs, name="out_proj_s")
    ssm_conv_s = jnp.swapaxes(newbuf_s, 0, 1)

    ffn_conv_p, ffn_conv_s = [], []

    def conv_ffn(i, xp, xs, final_gain):
        gain = norm_ffn[i:i + 1]
        cw, cb = ffn_conv_w[i], ffn_conv_b[i].reshape(1, f)
        tm = _row_tile(seq, 512)
        h_p, tail = ffn_up_prompt(xp, gain, w_up[i], cw, cb, seq, tm=tm, tn=f // 2, name=f"ffn_up_p{i}")
        xp = matmul_residual(h_p, w_down[i], xp, final_gain, name=f"ffn_down_p{i}")
        tiles_per_seq = seq // tm
        tail = tail.reshape(bp, tiles_per_seq, 8, f)[:, -1, 8 - (FFN_CONV_W - 1):, :]
        ffn_conv_p.append(tail)
        (gu,) = norm_matmul(xs, gain, [0], [w_up[i]], [[F32]], tn=f, name=f"ffn_up_s{i}")
        h_s, nb = ffn_gate_sample(gu, jnp.swapaxes(state_conv_ffn[i], 0, 1), cw, cb)
        xs = matmul_residual(h_s, w_down[i], xs, final_gain, name=f"ffn_down_s{i}")
        ffn_conv_s.append(jnp.swapaxes(nb, 0, 1))
        return xp, xs

    xp, xs = conv_ffn(0, xp, xs, None)

    gains = jnp.concatenate([norm_kv.reshape(1, d), norm_mix[1:2]], axis=0)
    kp, kp16, vp, vp16, qp16 = norm_matmul(
        xp, gains, [0, 0, 1], [wk, wv, wq], [[F32, BF16], [F32, BF16], [BF16]],
        scales=[1.0, 1.0, scale], tm=512, tn=attn_w, name="kvq_p")
    ks, vs, qs16 = norm_matmul(
        xs, gains, [0, 0, 1], [wk, wv, wq], [[F32], [F32], [BF16]],
        scales=[1.0, 1.0, scale], tn=attn_w, name="kvq_s")

    o_p = diff_attention_prompt(qp16.reshape(bp, seq, attn_w), kp16.reshape(bp, seq, attn_w),
                                vp16.reshape(bp, seq, attn_w), lvec, sub_w, lambda_init)
    xp = matmul_residual(o_p.reshape(mp, attn_w), wo, xp, name="attn_out_p")
    o_s = diff_attention_sample(qs16.reshape(bd, ATTN_HEADS, hw), ks.reshape(bd, ATTN_HEADS, hw),
                                vs.reshape(bd, ATTN_HEADS, hw), cache_k, cache_v, page_table,
                                lvec, sub_w, lambda_init)
    xs = matmul_residual(o_s.reshape(bd, attn_w), wo, xs, name="attn_out_s")

    xp, xs = conv_ffn(1, xp, xs, norm_final.reshape(1, d))

    return (xp.reshape(bp, seq, d), xs.reshape(bd, 1, d),
            ssm_state_p[None], ssm_state_s[None], ssm_conv_p[None], ssm_conv_s[None],
            jnp.stack(ffn_conv_p), jnp.stack(ffn_conv_s),
            kp.reshape(bp, seq, ATTN_HEADS, hw), vp.reshape(bp, seq, ATTN_HEADS, hw),
            ks.reshape(bd, 1, ATTN_HEADS, hw), vs.reshape(bd, 1, ATTN_HEADS, hw))
```

```python
import functools
import math

import jax
import jax.numpy as jnp
from jax import lax
from jax.experimental import pallas as pl
from jax.experimental.pallas import tpu as pltpu

F32 = jnp.float32
BF16 = jnp.bfloat16
EPS = 1e-5
LOG2E = 1.4426950408889634

SSM_HEAD_DIM = 64
SSM_GROUPS = 4
SSM_D_STATE = 128
SSM_CONV_W = 4
SSD_CHUNK = 128
ATTN_HEADS = 8
FFN_CONV_W = 3
PAGES_PER_STEP = 8
DEC_GROUP_PAGES = 2
DEC_SCORE_CHUNK = 16
DEC_PROB_CHUNK = 8

VMEM_LIMIT_BYTES = 56 * 1024 * 1024


def _cparams(*sem):
    return pltpu.CompilerParams(dimension_semantics=sem, vmem_limit_bytes=VMEM_LIMIT_BYTES)


def _silu(x):
    h = 0.5 * x
    return h + h * jnp.tanh(h)


def _rms_unit(x):
    return x * lax.rsqrt(jnp.mean(x * x, axis=-1, keepdims=True) + EPS)


def _split3(a):
    hi = a.astype(BF16)
    r = a - hi.astype(F32)
    mid = r.astype(BF16)
    lo = (r - mid.astype(F32)).astype(BF16)
    return hi, mid, lo


def _dot(a, b):
    return jnp.dot(a, b, preferred_element_type=F32)


def _dot_nt(a, b):
    return lax.dot_general(a, b, (((1,), (1,)), ((), ())), preferred_element_type=F32)


def _dot_exact_rhs(a, b_bf16):
    hi, mid, lo = _split3(a)
    return _dot(hi, b_bf16) + _dot(mid, b_bf16) + _dot(lo, b_bf16)


def _dot_exact_lhs(a_bf16, b):
    hi, mid, lo = _split3(b)
    return _dot(a_bf16, hi) + _dot(a_bf16, mid) + _dot(a_bf16, lo)


def _row_tile(m, pref):
    t = min(m, pref)
    assert m % t == 0, (m, t)
    return t


def _norm_mm_kernel(n_w, gain_idx, scales, out_dtypes, x_ref, g_ref, *rest):
    w_refs = rest[:n_w]
    o_refs = rest[n_w:-1]
    xn_ref = rest[-1]

    @pl.when(pl.program_id(1) == 0)
    def _():
        xh = _rms_unit(x_ref[...])
        for i in range(n_w):
            gi = gain_idx[i]
            xn_ref[i] = (xh * g_ref[gi:gi + 1, :]).astype(BF16)

    k = 0
    for i in range(n_w):
        r = _dot(xn_ref[i], w_refs[i][...])
        if scales[i] != 1.0:
            r = r * scales[i]
        for dt in out_dtypes[i]:
            o_refs[k][...] = r.astype(dt)
            k += 1


def norm_matmul(x, gains, gain_idx, weights, out_dtypes, scales=None, tm=512, tn=None, name="norm_mm"):
    m, kdim = x.shape
    n = weights[0].shape[1]
    n_w = len(weights)
    scales = tuple(scales) if scales is not None else (1.0,) * n_w
    tm = _row_tile(m, tm)
    tn = n if tn is None else tn
    assert n % tn == 0
    outs, specs = [], []
    for dts in out_dtypes:
        for dt in dts:
            outs.append(jax.ShapeDtypeStruct((m, n), dt))
            specs.append(pl.BlockSpec((tm, tn), lambda i, j: (i, j)))
    kern = functools.partial(_norm_mm_kernel, n_w, tuple(gain_idx), scales,
                             tuple(tuple(d) for d in out_dtypes))
    return pl.pallas_call(
        kern,
        grid=(m // tm, n // tn),
        in_specs=[pl.BlockSpec((tm, kdim), lambda i, j: (i, 0)),
                  pl.BlockSpec(gains.shape, lambda i, j: (0, 0))]
        + [pl.BlockSpec((kdim, tn), lambda i, j: (0, j)) for _ in weights],
        out_specs=specs,
        out_shape=outs,
        scratch_shapes=[pltpu.VMEM((n_w, tm, kdim), BF16)],
        compiler_params=_cparams("parallel", "arbitrary"),
        name=name,
    )(x, gains, *weights)


def _norm_mm_resident_kernel(tn, x_ref, g_ref, w_ref, o_ref):
    xn = (_rms_unit(x_ref[...]) * g_ref[...]).astype(BF16)
    for j in range(w_ref.shape[1] // tn):
        o_ref[:, j * tn:(j + 1) * tn] = _dot(xn, w_ref[:, j * tn:(j + 1) * tn])


def norm_matmul_resident(x, gain, w, tm=512, tn=768, name="norm_mm_res"):
    m, kdim = x.shape
    n = w.shape[1]
    tm = _row_tile(m, tm)
    assert n % tn == 0
    return pl.pallas_call(
        functools.partial(_norm_mm_resident_kernel, tn),
        grid=(m // tm,),
        in_specs=[pl.BlockSpec((tm, kdim), lambda i: (i, 0)),
                  pl.BlockSpec((1, kdim), lambda i: (0, 0)),
                  pl.BlockSpec((kdim, n), lambda i: (0, 0), pipeline_mode=pl.Buffered(1))],
        out_specs=pl.BlockSpec((tm, n), lambda i: (i, 0)),
        out_shape=jax.ShapeDtypeStruct((m, n), F32),
        compiler_params=_cparams("parallel"),
        name=name,
    )(x, gain, w)


def _mm_res_kernel(final_norm, a_ref, w_ref, r_ref, *rest):
    o_ref = rest[-1]
    y = r_ref[...] + _dot(a_ref[...], w_ref[...])
    if final_norm:
        y = _rms_unit(y) * rest[0][...]
    o_ref[...] = y


def matmul_residual(a, w, res, final_gain=None, tm=512, name="mm_res"):
    m, kdim = a.shape
    n = w.shape[1]
    tm = _row_tile(m, tm)
    ins = [a, w, res]
    specs = [pl.BlockSpec((tm, kdim), lambda i: (i, 0)),
             pl.BlockSpec((kdim, n), lambda i: (0, 0)),
             pl.BlockSpec((tm, n), lambda i: (i, 0))]
    if final_gain is not None:
        ins.append(final_gain)
        specs.append(pl.BlockSpec((1, n), lambda i: (0, 0)))
    return pl.pallas_call(
        functools.partial(_mm_res_kernel, final_gain is not None),
        grid=(m // tm,),
        in_specs=specs,
        out_specs=pl.BlockSpec((tm, n), lambda i: (i, 0)),
        out_shape=jax.ShapeDtypeStruct((m, n), F32),
        compiler_params=_cparams("parallel"),
        name=name,
    )(*ins)


HALO = 16


def _ffn_up_kernel(tm, seq, x_ref, halo_ref, g_ref, wg_ref, wu_ref, cw_ref, cb_ref,
                   h_ref, tail_ref, xn_ref):
    i = pl.program_id(0)

    @pl.when(pl.program_id(1) == 0)
    def _():
        gain = g_ref[...]
        xn_ref[HALO:, :] = (_rms_unit(x_ref[...]) * gain).astype(BF16)
        keep = jnp.where((i * tm) % seq == 0, 0.0, 1.0)
        xn_ref[0:HALO, :] = (_rms_unit(halo_ref[...]) * gain * keep).astype(BF16)

    g = _dot(xn_ref[...], wg_ref[...])
    u = _dot(xn_ref[HALO:, :], wu_ref[...])
    c = (cw_ref[2:3, :] * g + cw_ref[1:2, :] * pltpu.roll(g, 1, 0)
         + cw_ref[0:1, :] * pltpu.roll(g, 2, 0) + cb_ref[...])
    c = c[HALO:, :]
    h_ref[...] = (_silu(c) * u).astype(BF16)
    tail_ref[0] = g[tm + HALO - 8:, :]


def ffn_up_prompt(x, gain, w_up, cw, cb, seq, tm=512, tn=None, name="ffn_up"):
    m, d = x.shape
    f = cw.shape[1]
    tm = _row_tile(seq, tm)
    tn = f if tn is None else tn
    assert f % tn == 0 and tm % HALO == 0
    nj = f // tn
    hb = tm // HALO
    h, tail = pl.pallas_call(
        functools.partial(_ffn_up_kernel, tm, seq),
        grid=(m // tm, nj),
        in_specs=[pl.BlockSpec((tm, d), lambda i, j: (i, 0)),
                  pl.BlockSpec((HALO, d), lambda i, j: (jnp.maximum(i * hb - 1, 0), 0)),
                  pl.BlockSpec((1, d), lambda i, j: (0, 0)),
                  pl.BlockSpec((d, tn), lambda i, j: (0, j)),
                  pl.BlockSpec((d, tn), lambda i, j: (0, nj + j)),
                  pl.BlockSpec((FFN_CONV_W, tn), lambda i, j: (0, j)),
                  pl.BlockSpec((1, tn), lambda i, j: (0, j))],
        out_specs=[pl.BlockSpec((tm, tn), lambda i, j: (i, j)),
                   pl.BlockSpec((1, 8, tn), lambda i, j: (i, 0, j))],
        out_shape=[jax.ShapeDtypeStruct((m, f), BF16),
                   jax.ShapeDtypeStruct((m // tm, 8, f), F32)],
        scratch_shapes=[pltpu.VMEM((HALO + tm, d), BF16)],
        compiler_params=_cparams("parallel", "arbitrary"),
        name=name,
    )(x, x, gain, w_up, w_up, cw, cb)
    return h, tail


def _in_proj_kernel(tm, seq, d_inner, conv_dim, chunk, x_ref, halo_ref, g_ref, w_ref, cw_ref, cb_ref,
                    o_ref, tail_ref, xn_ref):
    i = pl.program_id(0)
    gain = g_ref[...]
    xn_ref[HALO:, :] = (_rms_unit(x_ref[...]) * gain).astype(BF16)
    keep = jnp.where((i * tm) % seq == 0, 0.0, 1.0)
    xn_ref[0:HALO, :] = (_rms_unit(halo_ref[...]) * gain * keep).astype(BF16)

    for c0 in range(0, d_inner, chunk):
        o_ref[:, c0:c0 + chunk] = _silu(_dot(xn_ref[HALO:, :], w_ref[:, c0:c0 + chunk]))
    for j0 in range(0, conv_dim, chunk):
        c0 = d_inner + j0
        g = _dot(xn_ref[...], w_ref[:, c0:c0 + chunk])
        conv = cb_ref[:, j0:j0 + chunk] + cw_ref[SSM_CONV_W - 1:SSM_CONV_W, j0:j0 + chunk] * g
        for k in range(1, SSM_CONV_W):
            conv = conv + cw_ref[SSM_CONV_W - 1 - k:SSM_CONV_W - k, j0:j0 + chunk] * pltpu.roll(g, k, 0)
        o_ref[:, c0:c0 + chunk] = _silu(conv[HALO:, :])
        tail_ref[0, :, j0:j0 + chunk] = g[tm + HALO - 8:, :]
    c0 = d_inner + conv_dim
    o_ref[:, c0:] = _dot(xn_ref[HALO:, :], w_ref[:, c0:])


def in_proj_prompt(x, gain, w, cw, cb, seq, d_inner, tm=512, name="in_proj_p"):
    m, kdim = x.shape
    n = w.shape[1]
    conv_dim = cw.shape[1]
    tm = _row_tile(seq, tm)
    chunk = math.gcd(math.gcd(d_inner, conv_dim), 1024)
    assert tm % HALO == 0 and chunk % 128 == 0 and n > d_inner + conv_dim
    hb = tm // HALO
    return pl.pallas_call(
        functools.partial(_in_proj_kernel, tm, seq, d_inner, conv_dim, chunk),
        grid=(m // tm,),
        in_specs=[pl.BlockSpec((tm, kdim), lambda i: (i, 0)),
                  pl.BlockSpec((HALO, kdim), lambda i: (jnp.maximum(i * hb - 1, 0), 0)),
                  pl.BlockSpec((1, kdim), lambda i: (0, 0)),
                  pl.BlockSpec((kdim, n), lambda i: (0, 0), pipeline_mode=pl.Buffered(1)),
                  pl.BlockSpec(cw.shape, lambda i: (0, 0)),
                  pl.BlockSpec(cb.shape, lambda i: (0, 0))],
        out_specs=[pl.BlockSpec((tm, n), lambda i: (i, 0)),
                   pl.BlockSpec((1, 8, conv_dim), lambda i: (i, 0, 0))],
        out_shape=[jax.ShapeDtypeStruct((m, n), F32),
                   jax.ShapeDtypeStruct((m // tm, 8, conv_dim), F32)],
        scratch_shapes=[pltpu.VMEM((HALO + tm, kdim), BF16)],
        compiler_params=_cparams("parallel"),
        name=name,
    )(x, x, gain, w, cw, cb)


def _ssd_kernel(d_inner, n_heads, zx_ref, dtb_ref, alog_ref, dskip_ref, nw_ref,
                expand_ref, g_ref, state_ref, ht_ref, y_ref):
    q = SSD_CHUNK
    ns = SSM_D_STATE
    gn = SSM_GROUPS * ns
    conv_dim = d_inner + 2 * gn
    gw = d_inner // SSM_GROUPS
    c = pl.program_id(1)
    last = pl.num_programs(1) - 1

    @pl.when(c == 0)
    def _():
        ht_ref[...] = jnp.zeros(ht_ref.shape, F32)

    xs = zx_ref[:, d_inner:2 * d_inner]
    bm = zx_ref[:, 2 * d_inner:2 * d_inner + gn]
    cm = zx_ref[:, 2 * d_inner + gn:d_inner + conv_dim]

    dt_raw = zx_ref[:, d_inner + conv_dim:d_inner + conv_dim + 128] + dtb_ref[...]
    dt = jnp.maximum(dt_raw, 0.0) + jnp.log1p(jnp.exp(-jnp.abs(dt_raw)))
    da = dt * (-jnp.exp(alog_ref[...]))
    row = lax.broadcasted_iota(jnp.int32, (q, q), 0)
    col = lax.broadcasted_iota(jnp.int32, (q, q), 1)
    tri = row >= col
    a_cum = _dot_exact_lhs(jnp.where(tri, 1.0, 0.0).astype(BF16), da)
    a_cum_t = jnp.transpose(a_cum)

    expand = expand_ref[...]
    dt_hp = _dot_exact_rhs(dt, expand)
    a_hp = _dot_exact_rhs(a_cum, expand)
    xdt = xs * dt_hp
    a_last = a_hp[q - 1:q, :]
    xdt_b = xdt.astype(BF16)
    dec_x = (jnp.exp(a_last - a_hp) * xdt).astype(BF16)
    off_scale = jnp.exp(a_hp)
    chunk_decay = jnp.exp(a_last)

    lane = lax.broadcasted_iota(jnp.int32, (q, 2 * SSM_HEAD_DIM), 1)
    heads_per_group = n_heads // SSM_GROUPS
    for g in range(SSM_GROUPS):
        bg = bm[:, g * ns:(g + 1) * ns]
        cg = cm[:, g * ns:(g + 1) * ns].astype(BF16)
        cb_mat = _dot_nt(cg, bg.astype(BF16))
        cols = slice(g * gw, (g + 1) * gw)
        ht = ht_ref[g]
        y_off = _dot(cg, ht.astype(BF16)) * off_scale[:, cols]
        new_states = _dot(jnp.transpose(bg).astype(BF16), dec_x[:, cols])
        ht_ref[g] = ht * chunk_decay[:, cols] + new_states
        for pr in range(heads_per_group // 2):
            c0 = g * gw + pr * 2 * SSM_HEAD_DIM
            x2 = xdt_b[:, c0:c0 + 2 * SSM_HEAD_DIM]
            yd = None
            for half in range(2):
                hh = g * heads_per_group + pr * 2 + half
                seg = a_cum[:, hh:hh + 1] - a_cum_t[hh:hh + 1, :]
                lmat = jnp.where(tri, jnp.exp(jnp.minimum(seg, 0.0)), 0.0)
                mm = (cb_mat * lmat).astype(BF16)
                xh = jnp.where((lane >= SSM_HEAD_DIM) == (half == 1), x2, jnp.zeros_like(x2))
                t = _dot(mm, xh)
                yd = t if yd is None else yd + t
            y_ref[:, c0:c0 + 2 * SSM_HEAD_DIM] = yd + y_off[:, pr * 2 * SSM_HEAD_DIM:(pr + 1) * 2 * SSM_HEAD_DIM]

    y = y_ref[...] + dskip_ref[...] * xs
    gated = y * zx_ref[:, 0:d_inner]
    for g in range(SSM_GROUPS):
        cols = slice(g * gw, (g + 1) * gw)
        g_ref[:, cols] = (_rms_unit(gated[:, cols]) * nw_ref[:, cols]).astype(BF16)

    @pl.when(c == last)
    def _():
        for g in range(SSM_GROUPS):
            st = jnp.transpose(ht_ref[g])
            state_ref[0, g * heads_per_group:(g + 1) * heads_per_group] = st.reshape(
                heads_per_group, SSM_HEAD_DIM, ns)


def ssd_prompt(zx, batch, seq, d_inner, dtb, alog, dskip_hp, nw, expand, name="ssd"):
    m, width = zx.shape
    n_heads = d_inner // SSM_HEAD_DIM
    nc = seq // SSD_CHUNK
    q = SSD_CHUNK
    full = lambda shape: pl.BlockSpec(shape, lambda b, c: (0,) * len(shape))
    return pl.pallas_call(
        functools.partial(_ssd_kernel, d_inner, n_heads),
        grid=(batch, nc),
        in_specs=[pl.BlockSpec((q, width), lambda b, c: (b * nc + c, 0)),
                  full(dtb.shape), full(alog.shape),
                  full(dskip_hp.shape), full(nw.shape), full(expand.shape)],
        out_specs=[pl.BlockSpec((q, d_inner), lambda b, c: (b * nc + c, 0)),
                   pl.BlockSpec((1, n_heads, SSM_HEAD_DIM, SSM_D_STATE), lambda b, c: (b, 0, 0, 0))],
        out_shape=[jax.ShapeDtypeStruct((m, d_inner), BF16),
                   jax.ShapeDtypeStruct((batch, n_heads, SSM_HEAD_DIM, SSM_D_STATE), F32)],
        scratch_shapes=[pltpu.VMEM((SSM_GROUPS, SSM_D_STATE, d_inner // SSM_GROUPS), F32),
                        pltpu.VMEM((q, d_inner), F32)],
        compiler_params=_cparams("parallel", "arbitrary"),
        name=name,
    )(zx, dtb, alog, dskip_hp, nw, expand)


def _diff_lambda(lv, lambda_init):
    s1 = jnp.sum(lv[0:1, :] * lv[1:2, :], axis=-1, keepdims=True)
    s2 = jnp.sum(lv[2:3, :] * lv[3:4, :], axis=-1, keepdims=True)
    return jnp.exp(s1) - jnp.exp(s2) + lambda_init


ATTN_ROW_CHUNK = 64
ATTN_HEADS_PER_STEP = 2


def _attn_kernel(tq, n_hps, lambda_init, q_ref, k_ref, v_ref, lv_ref, sub_ref, o_ref, *scratch):
    qi = pl.program_id(2)
    hw = q_ref.shape[2] // n_hps
    hd = hw // 2
    rc = ATTN_ROW_CHUNK
    per_head = len(scratch) // n_hps
    heads = [scratch[h * per_head:(h + 1) * per_head] for h in range(n_hps)]
    ones = jnp.ones((tq, hw), BF16)

    for h, (q2_ref, _, _, _, m_ref, _, acc_ref) in enumerate(heads):
        qh = q_ref[0, :, h * hw:(h + 1) * hw]
        lane = lax.broadcasted_iota(jnp.int32, qh.shape, 1)
        zero = jnp.zeros_like(qh)
        q2_ref[0:tq, :] = jnp.where(lane < hd, qh, zero)
        q2_ref[tq:, :] = jnp.where(lane >= hd, qh, zero)
        m_ref[...] = jnp.full(m_ref.shape, -jnp.inf, F32)
        acc_ref[...] = jnp.zeros(acc_ref.shape, F32)

    def qk(h, kb, odd):
        q2_ref = heads[h][0]
        s_ref = heads[h][2 if odd else 1]
        start = pl.multiple_of(kb * tq, tq)
        s_ref[...] = _dot_nt(q2_ref[...], k_ref[0, pl.ds(start, tq), h * hw:(h + 1) * hw])

    def softmax_pv(h, kb, odd, masked):
        _, s0_ref, s1_ref, p_ref, m_ref, c_ref, acc_ref = heads[h]
        s_ref = s1_ref if odd else s0_ref
        start = pl.multiple_of(kb * tq, tq)
        for r0 in range(0, 2 * tq, rc):
            rows = slice(r0, r0 + rc)
            s = s_ref[rows, :]
            if masked:
                r = lax.broadcasted_iota(jnp.int32, (rc, tq), 0) + (r0 % tq)
                cidx = lax.broadcasted_iota(jnp.int32, (rc, tq), 1)
                s = jnp.where(r >= cidx, s, -jnp.inf)
            m_old = m_ref[rows, :]
            m_new = jnp.maximum(m_old, jnp.max(s, axis=-1, keepdims=True))
            m_ref[rows, :] = m_new
            c_ref[rows, :] = jnp.exp(m_old - m_new)
            p_ref[rows, :] = jnp.exp(s - m_new).astype(BF16)
        v_ext = jnp.concatenate([v_ref[0, pl.ds(start, tq), h * hw:(h + 1) * hw], ones], axis=1)
        pv = _dot(p_ref[...], v_ext)
        for r0 in range(0, 2 * tq, rc):
            rows = slice(r0, r0 + rc)
            acc_ref[rows, :] = acc_ref[rows, :] * c_ref[rows, :] + pv[rows, :]

    def each_head(fn, *args):
        for h in range(n_hps):
            fn(h, *args)

    each_head(qk, 0, False)

    def body(kk, carry):
        kb = 2 * kk
        each_head(qk, kb + 1, True)
        each_head(softmax_pv, kb, False, False)
        each_head(qk, kb + 2, False)
        each_head(softmax_pv, kb + 1, True, False)
        return carry

    lax.fori_loop(0, qi // 2, body, 0)

    @pl.when(qi % 2 == 0)
    def _():
        each_head(softmax_pv, qi, False, True)

    @pl.when(qi % 2 == 1)
    def _():
        each_head(qk, qi, True)
        each_head(softmax_pv, qi - 1, False, False)
        each_head(softmax_pv, qi, True, True)

    lam = _diff_lambda(lv_ref[...], lambda_init)
    for h in range(n_hps):
        acc = heads[h][6][...]
        o = acc[:, :hw] / acc[:, hw:]
        o = o[:tq] - lam * o[tq:]
        o = _rms_unit(o) * sub_ref[...]
        o_ref[0, :, h * hw:(h + 1) * hw] = (o * (1.0 - lambda_init)).astype(BF16)


def diff_attention_prompt(q, k, v, lvec, subln, lambda_init, tq=512, name="diff_attn"):
    b, l, width = q.shape
    hw = width // ATTN_HEADS
    n_hps = ATTN_HEADS_PER_STEP if ATTN_HEADS % ATTN_HEADS_PER_STEP == 0 else 1
    bw = n_hps * hw
    tq = _row_tile(l, tq)
    assert tq % ATTN_ROW_CHUNK == 0
    head_scratch = [pltpu.VMEM((2 * tq, hw), BF16),
                    pltpu.VMEM((2 * tq, tq), F32),
                    pltpu.VMEM((2 * tq, tq), F32),
                    pltpu.VMEM((2 * tq, tq), BF16),
                    pltpu.VMEM((2 * tq, 1), F32),
                    pltpu.VMEM((2 * tq, 1), F32),
                    pltpu.VMEM((2 * tq, 2 * hw), F32)]
    return pl.pallas_call(
        functools.partial(_attn_kernel, tq, n_hps, lambda_init),
        grid=(b, ATTN_HEADS // n_hps, l // tq),
        in_specs=[pl.BlockSpec((1, tq, bw), lambda bi, h, i: (bi, i, h)),
                  pl.BlockSpec((1, l, bw), lambda bi, h, i: (bi, 0, h)),
                  pl.BlockSpec((1, l, bw), lambda bi, h, i: (bi, 0, h)),
                  pl.BlockSpec(lvec.shape, lambda bi, h, i: (0, 0)),
                  pl.BlockSpec(subln.shape, lambda bi, h, i: (0, 0))],
        out_specs=pl.BlockSpec((1, tq, bw), lambda bi, h, i: (bi, i, h)),
        out_shape=jax.ShapeDtypeStruct((b, l, width), BF16),
        scratch_shapes=head_scratch * n_hps,
        compiler_params=_cparams("parallel", "parallel", "arbitrary"),
        name=name,
    )(q, k, v, lvec, subln)


def _attn_flat_kernel(tq, n_hps, nq, lambda_init, q_ref, k_ref, v_ref, lv_ref, sub_ref, o_ref, *scratch):
    hw = q_ref.shape[2] // n_hps
    hd = hw // 2
    rc = ATTN_ROW_CHUNK
    per_head = len(scratch) // n_hps
    heads = []
    for h in range(n_hps):
        r = scratch[h * per_head:(h + 1) * per_head]
        heads.append(dict(q2=r[0:2], f=r[2:4], s=r[4:6], p=r[6], m=r[7], c=r[8], acc=r[9],
                          cols=slice(h * hw, (h + 1) * hw)))
    ones = jnp.ones((tq, hw), BF16)

    def rows_of(tile):
        if isinstance(tile, int):
            return pl.ds(tile * tq, tq)
        return pl.ds(pl.multiple_of(tile * tq, tq), tq)

    def build_q2(hd_, tile, e):
        qh = q_ref[0, rows_of(tile), hd_["cols"]]
        lane = lax.broadcasted_iota(jnp.int32, qh.shape, 1)
        zero = jnp.zeros_like(qh)
        hd_["q2"][e][0:tq, :] = jnp.where(lane < hd, qh, zero)
        hd_["q2"][e][tq:, :] = jnp.where(lane >= hd, qh, zero)

    def init_tile(hd_):
        hd_["m"][...] = jnp.full(hd_["m"].shape, -jnp.inf, F32)
        hd_["acc"][...] = jnp.zeros(hd_["acc"].shape, F32)

    def qk(hd_, e, kb, dst):
        dst[...] = _dot_nt(hd_["q2"][e][...], k_ref[0, rows_of(kb), hd_["cols"]])

    def softmax_pv(hd_, kb, src, masked):
        m_ref, c_ref, p_ref, acc_ref = hd_["m"], hd_["c"], hd_["p"], hd_["acc"]
        for r0 in range(0, 2 * tq, rc):
            rows = slice(r0, r0 + rc)
            s = src[rows, :]
            if masked:
                r = lax.broadcasted_iota(jnp.int32, (rc, tq), 0) + (r0 % tq)
                cidx = lax.broadcasted_iota(jnp.int32, (rc, tq), 1)
                s = jnp.where(r >= cidx, s, -jnp.inf)
            m_old = m_ref[rows, :]
            m_new = jnp.maximum(m_old, jnp.max(s, axis=-1, keepdims=True))
            m_ref[rows, :] = m_new
            c_ref[rows, :] = jnp.exp(m_old - m_new)
            p_ref[rows, :] = jnp.exp(s - m_new).astype(BF16)
        v_ext = jnp.concatenate([v_ref[0, rows_of(kb), hd_["cols"]], ones], axis=1)
        pv = _dot(p_ref[...], v_ext)
        for r0 in range(0, 2 * tq, rc):
            rows = slice(r0, r0 + rc)
            acc_ref[rows, :] = acc_ref[rows, :] * c_ref[rows, :] + pv[rows, :]

    lam = _diff_lambda(lv_ref[...], lambda_init)

    def finalize(hd_, tile):
        acc = hd_["acc"][...]
        o = acc[:, :hw] / acc[:, hw:]
        o = o[:tq] - lam * o[tq:]
        o = _rms_unit(o) * sub_ref[...]
        o_ref[0, rows_of(tile), hd_["cols"]] = (o * (1.0 - lambda_init)).astype(BF16)

    def each(fn, *args):
        for hd_ in heads:
            fn(hd_, *args)

    def start_next(tile, e):
        nxt = jnp.minimum(tile + 1, nq - 1)
        each(build_q2, nxt, 1 - e)
        for hd_ in heads:
            qk(hd_, 1 - e, 0, hd_["f"][1 - e])

    def tile_body(tile, e, only_diagonal):
        each(init_tile)
        if only_diagonal:
            start_next(tile, e)
            for hd_ in heads:
                softmax_pv(hd_, 0, hd_["f"][e], True)
        else:
            for hd_ in heads:
                qk(hd_, e, 1, hd_["s"][0])
            for hd_ in heads:
                softmax_pv(hd_, 0, hd_["f"][e], False)

            def body(kk, carry):
                kb = 1 + 2 * kk
                for hd_ in heads:
                    qk(hd_, e, kb + 1, hd_["s"][1])
                for hd_ in heads:
                    softmax_pv(hd_, kb, hd_["s"][0], False)
                for hd_ in heads:
                    qk(hd_, e, kb + 2, hd_["s"][0])
                for hd_ in heads:
                    softmax_pv(hd_, kb + 1, hd_["s"][1], False)
                return carry

            lax.fori_loop(0, (tile - 1) // 2, body, 0)
            if e == 0:
                for hd_ in heads:
                    qk(hd_, e, tile, hd_["s"][1])
                for hd_ in heads:
                    softmax_pv(hd_, tile - 1, hd_["s"][0], False)
                start_next(tile, e)
                for hd_ in heads:
                    softmax_pv(hd_, tile, hd_["s"][1], True)
            else:
                start_next(tile, e)
                for hd_ in heads:
                    softmax_pv(hd_, tile, hd_["s"][0], True)
        each(finalize, tile)

    each(build_q2, 0, 0)
    for hd_ in heads:
        qk(hd_, 0, 0, hd_["f"][0])

    def pair_body(t, carry):
        @pl.when(t == 0)
        def _():
            tile_body(0, 0, True)

        @pl.when(t > 0)
        def _():
            tile_body(2 * t, 0, False)

        tile_body(2 * t + 1, 1, False)
        return carry

    lax.fori_loop(0, nq // 2, pair_body, 0)


def diff_attention_prompt_flat(q, k, v, lvec, subln, lambda_init, tq=512, name="diff_attn"):
    b, l, width = q.shape
    hw = width // ATTN_HEADS
    n_hps = ATTN_HEADS_PER_STEP if ATTN_HEADS % ATTN_HEADS_PER_STEP == 0 else 1
    bw = n_hps * hw
    tq = _row_tile(l, tq)
    nq = l // tq
    assert tq % ATTN_ROW_CHUNK == 0 and nq % 2 == 0
    head_scratch = ([pltpu.VMEM((2 * tq, hw), BF16)] * 2
                    + [pltpu.VMEM((2 * tq, tq), F32)] * 4
                    + [pltpu.VMEM((2 * tq, tq), BF16),
                       pltpu.VMEM((2 * tq, 1), F32),
                       pltpu.VMEM((2 * tq, 1), F32),
                       pltpu.VMEM((2 * tq, 2 * hw), F32)])
    seq_spec = pl.BlockSpec((1, l, bw), lambda bi, h: (bi, 0, h))
    return pl.pallas_call(
        functools.partial(_attn_flat_kernel, tq, n_hps, nq, lambda_init),
        grid=(b, ATTN_HEADS // n_hps),
        in_specs=[seq_spec, seq_spec, seq_spec,
                  pl.BlockSpec(lvec.shape, lambda bi, h: (0, 0)),
                  pl.BlockSpec(subln.shape, lambda bi, h: (0, 0))],
        out_specs=seq_spec,
        out_shape=jax.ShapeDtypeStruct((b, l, width), BF16),
        scratch_shapes=head_scratch * n_hps,
        compiler_params=_cparams("parallel", "parallel"),
        name=name,
    )(q, k, v, lvec, subln)


def _ssm_prep_kernel(d_inner, zx_ref, buf_ref, cw_ref, cb_ref, dtb_ref, alog_ref, expand_ref,
                     newbuf_ref, xs_ref, xdt_ref, b_ref, c_ref, decay_ref):
    gn = SSM_GROUPS * SSM_D_STATE
    conv_dim = d_inner + 2 * gn
    xnew = zx_ref[:, d_inner:d_inner + conv_dim]
    conv = cb_ref[...] + cw_ref[SSM_CONV_W - 1:SSM_CONV_W, :] * xnew
    for k in range(SSM_CONV_W - 1):
        conv = conv + cw_ref[k:k + 1, :] * buf_ref[k]
    for k in range(SSM_CONV_W - 2):
        newbuf_ref[k] = buf_ref[k + 1]
    newbuf_ref[SSM_CONV_W - 2] = xnew
    xbc = _silu(conv)
    xs = xbc[:, :d_inner]
    dt_raw = zx_ref[:, d_inner + conv_dim:d_inner + conv_dim + 128] + dtb_ref[...]
    dt = jnp.maximum(dt_raw, 0.0) + jnp.log1p(jnp.exp(-jnp.abs(dt_raw)))
    decay_ref[...] = jnp.exp(dt * (-jnp.exp(alog_ref[...])))
    xs_ref[...] = xs
    xdt_ref[...] = xs * _dot_exact_rhs(dt, expand_ref[...])
    b_ref[...] = xbc[:, d_inner:d_inner + gn]
    c_ref[...] = xbc[:, d_inner + gn:]


def _col_bcast(rowvec):
    n = rowvec.shape[1]
    wide = jnp.broadcast_to(rowvec, (128, n))
    return jnp.concatenate([jnp.transpose(wide[:, j * 128:(j + 1) * 128]) for j in range(n // 128)], axis=0)


def _ssm_state_kernel(n_heads, h0_ref, xdt_ref, decay_ref, b_ref, c_ref, hn_ref, y_ref):
    ns = SSM_D_STATE
    hpg = n_heads // SSM_GROUPS
    xcol = _col_bcast(xdt_ref[0])
    dcol = _col_bcast(decay_ref[0])
    for g in range(SSM_GROUPS):
        brow = b_ref[0, :, g * ns:(g + 1) * ns]
        crow = jnp.broadcast_to(c_ref[0, :, g * ns:(g + 1) * ns], (8, ns))
        chi, cmid, _ = _split3(crow)
        parts = []
        for r in range(hpg):
            h = g * hpg + r
            hn = (h0_ref[0, h] * dcol[h:h + 1, :]
                  + xcol[h * SSM_HEAD_DIM:(h + 1) * SSM_HEAD_DIM, :] * brow)
            hn_ref[0, h] = hn
            parts.append(hn)
        hhi, hmid, _ = _split3(jnp.concatenate(parts, axis=0))
        yr = _dot_nt(chi, hhi) + _dot_nt(cmid, hhi) + _dot_nt(chi, hmid)
        y_ref[0, :, g * hpg * SSM_HEAD_DIM:(g + 1) * hpg * SSM_HEAD_DIM] = yr[0:1, :]


def _ssm_post_kernel(d_inner, y_ref, xs_ref, zx_ref, dskip_ref, nw_ref, g_ref):
    gw = d_inner // SSM_GROUPS
    y = y_ref[...] + dskip_ref[...] * xs_ref[...]
    gated = y * _silu(zx_ref[:, 0:d_inner])
    for g in range(SSM_GROUPS):
        cols = slice(g * gw, (g + 1) * gw)
        g_ref[:, cols] = (_rms_unit(gated[:, cols]) * nw_ref[:, cols]).astype(BF16)


def ssm_step_sample(zx, conv_buf_t, h0, d_inner, cw, cb, dtb, alog, dskip_hp, nw, expand):
    bsz = zx.shape[0]
    n_heads = d_inner // SSM_HEAD_DIM
    gn = SSM_GROUPS * SSM_D_STATE
    conv_dim = d_inner + 2 * gn
    newbuf, xs, xdt, bm, cm, decay = pl.pallas_call(
        functools.partial(_ssm_prep_kernel, d_inner),
        out_shape=[jax.ShapeDtypeStruct(conv_buf_t.shape, F32),
                   jax.ShapeDtypeStruct((bsz, d_inner), F32),
                   jax.ShapeDtypeStruct((bsz, d_inner), F32),
                   jax.ShapeDtypeStruct((bsz, gn), F32),
                   jax.ShapeDtypeStruct((bsz, gn), F32),
                   jax.ShapeDtypeStruct((bsz, 128), F32)],
        compiler_params=pltpu.CompilerParams(vmem_limit_bytes=VMEM_LIMIT_BYTES),
        name="ssm_prep",
    )(zx, conv_buf_t, cw, cb, dtb, alog, expand)
    row3 = lambda a: a.reshape(bsz, 1, a.shape[1])
    hn, y = pl.pallas_call(
        functools.partial(_ssm_state_kernel, n_heads),
        grid=(bsz,),
        in_specs=[pl.BlockSpec((1, n_heads, SSM_HEAD_DIM, SSM_D_STATE), lambda b: (b, 0, 0, 0)),
                  pl.BlockSpec((1, 1, d_inner), lambda b: (b, 0, 0)),
                  pl.BlockSpec((1, 1, 128), lambda b: (b, 0, 0)),
                  pl.BlockSpec((1, 1, gn), lambda b: (b, 0, 0)),
                  pl.BlockSpec((1, 1, gn), lambda b: (b, 0, 0))],
        out_specs=[pl.BlockSpec((1, n_heads, SSM_HEAD_DIM, SSM_D_STATE), lambda b: (b, 0, 0, 0)),
                   pl.BlockSpec((1, 1, d_inner), lambda b: (b, 0, 0))],
        out_shape=[jax.ShapeDtypeStruct(h0.shape, F32),
                   jax.ShapeDtypeStruct((bsz, 1, d_inner), F32)],
        compiler_params=_cparams("parallel"),
        name="ssm_state",
    )(h0, row3(xdt), row3(decay), row3(bm), row3(cm))
    gact = pl.pallas_call(
        functools.partial(_ssm_post_kernel, d_inner),
        out_shape=jax.ShapeDtypeStruct((bsz, d_inner), BF16),
        compiler_params=pltpu.CompilerParams(vmem_limit_bytes=VMEM_LIMIT_BYTES),
        name="ssm_post",
    )(y.reshape(bsz, d_inner), xs, zx, dskip_hp, nw)
    return gact, newbuf, hn


def _ffn_gate_sample_kernel(f, gu_ref, buf_ref, cw_ref, cb_ref, h_ref, newbuf_ref):
    g = gu_ref[:, 0:f]
    u = gu_ref[:, f:2 * f]
    c = cb_ref[...] + cw_ref[FFN_CONV_W - 1:FFN_CONV_W, :] * g
    for k in range(FFN_CONV_W - 1):
        c = c + cw_ref[k:k + 1, :] * buf_ref[k]
    for k in range(FFN_CONV_W - 2):
        newbuf_ref[k] = buf_ref[k + 1]
    newbuf_ref[FFN_CONV_W - 2] = g
    h_ref[...] = (_silu(c) * u).astype(BF16)


def ffn_gate_sample(gu, buf_t, cw, cb):
    bsz = gu.shape[0]
    f = cw.shape[1]
    return pl.pallas_call(
        functools.partial(_ffn_gate_sample_kernel, f),
        out_shape=[jax.ShapeDtypeStruct((bsz, f), BF16),
                   jax.ShapeDtypeStruct(buf_t.shape, F32)],
        compiler_params=pltpu.CompilerParams(vmem_limit_bytes=VMEM_LIMIT_BYTES),
        name="ffn_gate_sample",
    )(gu, buf_t, cw, cb)


def _dec_attn_kernel(npg, lambda_init, pt_ref, q_ref, kn_ref, vn_ref, lv_ref, sub_ref, *rest):
    k_refs = rest[:npg]
    v_refs = rest[npg:2 * npg]
    o_ref = rest[2 * npg]
    s_ref, m_ref, l_ref, acc_ref = rest[2 * npg + 1:]
    del pt_ref
    j = pl.program_id(1)
    page, nh, hw = k_refs[0].shape[1:]
    hd = hw // 2
    sc, pc = DEC_SCORE_CHUNK, DEC_PROB_CHUNK

    qv = q_ref[0].astype(F32) * LOG2E
    srow = lax.broadcasted_iota(jnp.int32, (hw, hw), 0)
    scol = lax.broadcasted_iota(jnp.int32, (hw, hw), 1)
    seg = jnp.where(srow // hd == scol // hd, 1.0, 0.0).astype(BF16)

    def scores(k_tok):
        n_tok = k_tok.shape[0]
        prod = (k_tok * qv[None]).reshape(n_tok * nh, hw).astype(BF16)
        return _dot(prod, seg).reshape(n_tok, nh, hw)

    def swap_halves(x):
        return pltpu.roll(x.reshape(-1, hw), hd, 1).reshape(x.shape)

    @pl.when(j == 0)
    def _():
        m_ref[...] = scores(kn_ref[...])[0]
        l_ref[...] = jnp.ones(l_ref.shape, F32)
        vn = vn_ref[0]
        acc_ref[...] = jnp.concatenate([vn, vn], axis=1)

    gp = DEC_GROUP_PAGES
    n_groups = npg // gp

    def score_chunks(g):
        return [(p, t0) for p in range(g * gp, (g + 1) * gp) for t0 in range(0, page, sc)]

    def prob_chunks(g):
        return [(p, t0) for p in range(g * gp, (g + 1) * gp) for t0 in range(0, page, pc)]

    def score_chunk(p, t0, m_run):
        s = scores(k_refs[p][0, t0:t0 + sc])
        s_ref[p, t0:t0 + sc] = s
        return jnp.maximum(m_run, jnp.max(s, axis=0))

    m_cur = m_ref[...]
    l_run = l_ref[...]
    acc = acc_ref[...]
    acc_a, acc_b = acc[:, :hw], acc[:, hw:]
    m_next = m_cur
    for p, t0 in score_chunks(0):
        m_next = score_chunk(p, t0, m_next)
    for g in range(n_groups):
        m_new = m_next
        corr = jnp.exp2(m_cur - m_new)
        l_run = l_run * corr
        acc_a = acc_a * corr
        acc_b = acc_b * swap_halves(corr)
        m_cur = m_new
        pending = score_chunks(g + 1) if g + 1 < n_groups else []
        every = sc // pc
        for i, (p, t0) in enumerate(prob_chunks(g)):
            if pending and i % every == 0:
                m_next = score_chunk(*pending.pop(0), m_next)
            pr = jnp.exp2(s_ref[p, t0:t0 + pc] - m_new[None])
            l_run = l_run + jnp.sum(pr, axis=0)
            vp = v_refs[p][0, t0:t0 + pc]
            acc_a = acc_a + jnp.sum(pr * vp, axis=0)
            acc_b = acc_b + jnp.sum(swap_halves(pr) * vp, axis=0)
        assert not pending
    m_ref[...] = m_cur
    l_ref[...] = l_run
    acc_ref[...] = jnp.concatenate([acc_a, acc_b], axis=1)

    @pl.when(j == pl.num_programs(1) - 1)
    def _():
        xa = acc_a / l_run
        xb = acc_b / swap_halves(l_run)
        lam = _diff_lambda(lv_ref[...], lambda_init)
        low = lax.broadcasted_iota(jnp.int32, (nh, hw), 1) < hd
        o = jnp.where(low, xa - lam * xb, xb - lam * xa)
        o = _rms_unit(o) * sub_ref[...]
        o_ref[0] = (o * (1.0 - lambda_init)).astype(BF16)


def diff_attention_sample(q, k_new, v_new, cache_k, cache_v, page_table, lvec, subln, lambda_init):
    bsz, nh, hw = q.shape
    n_pages = page_table.shape[1]
    page = cache_k.shape[1]
    npg = min(PAGES_PER_STEP, n_pages)
    assert n_pages % npg == 0
    row = pl.BlockSpec((1, nh, hw), lambda b, j, pt: (b, 0, 0))

    def page_spec(p):
        return pl.BlockSpec((1, page, nh, hw), lambda b, j, pt: (pt[b, j * npg + p], 0, 0, 0))

    grid_spec = pltpu.PrefetchScalarGridSpec(
        num_scalar_prefetch=1,
        grid=(bsz, n_pages // npg),
        in_specs=[row, row, row,
                  pl.BlockSpec(lvec.shape, lambda b, j, pt: (0, 0)),
                  pl.BlockSpec(subln.shape, lambda b, j, pt: (0, 0))]
        + [page_spec(p) for p in range(npg)] + [page_spec(p) for p in range(npg)],
        out_specs=row,
        scratch_shapes=[pltpu.VMEM((npg, page, nh, hw), F32),
                        pltpu.VMEM((nh, hw), F32),
                        pltpu.VMEM((nh, hw), F32),
                        pltpu.VMEM((nh, 2 * hw), F32)],
    )
    return pl.pallas_call(
        functools.partial(_dec_attn_kernel, npg, lambda_init),
        grid_spec=grid_spec,
        out_shape=jax.ShapeDtypeStruct((bsz, nh, hw), BF16),
        compiler_params=_cparams("parallel", "arbitrary"),
        name="diff_attn_sample",
    )(page_table, q, k_new, v_new, lvec, subln, *([cache_k] * npg), *([cache_v] * npg))


def kernel(x_prompt, x_sample, state_ssm, state_conv_ssm, state_conv_ffn, cache_k, cache_v, page_table,
           norm_mix, norm_ffn, ssm_in_proj, ssm_conv_w, ssm_conv_b, ssm_dt_bias, ssm_a_log, ssm_d,
           ssm_norm, ssm_out_proj, ffn_up, ffn_conv_w, ffn_conv_b, ffn_down, norm_kv, w_k, w_v,
           w_q, lambda_q1, lambda_k1, lambda_q2, lambda_k2, subln, w_o, norm_final):
    bp, seq, d = x_prompt.shape
    bd = x_sample.shape[0]
    depth = norm_mix.shape[0]
    assert depth == 2 and ssm_in_proj.shape[0] == 1 and x_sample.shape[1] == 1
    d_inner = ssm_norm.shape[1]
    n_heads = ssm_dt_bias.shape[1]
    assert n_heads * SSM_HEAD_DIM == d_inner and n_heads <= 128
    conv_dim = ssm_conv_w.shape[2]
    f = ffn_conv_w.shape[2]
    attn_w = w_k.shape[1]
    hw = attn_w // ATTN_HEADS
    mp = bp * seq

    xp = x_prompt.reshape(mp, d)
    xs = x_sample.reshape(bd, d)

    in_dim = ssm_in_proj.shape[2]
    in_pad = -(-(d_inner + conv_dim + 128) // 768) * 768
    w_in = jnp.pad(ssm_in_proj[0], ((0, 0), (0, in_pad - in_dim))).astype(BF16)
    w_out = ssm_out_proj[0].astype(BF16)
    w_up = [ffn_up[i].astype(BF16) for i in range(depth)]
    w_down = [ffn_down[i].astype(BF16) for i in range(depth)]
    wk, wv, wq, wo = w_k.astype(BF16), w_v.astype(BF16), w_q[0].astype(BF16), w_o[0].astype(BF16)
    dtb = jnp.pad(ssm_dt_bias[0], (0, 128 - n_heads)).reshape(1, 128)
    alog = jnp.pad(ssm_a_log[0], (0, 128 - n_heads)).reshape(1, 128)
    dskip_hp = jnp.repeat(ssm_d[0], SSM_HEAD_DIM).reshape(1, d_inner)
    ssm_nw = ssm_norm[0].reshape(1, d_inner)
    expand = (lax.broadcasted_iota(jnp.int32, (128, d_inner), 0)
              == lax.broadcasted_iota(jnp.int32, (128, d_inner), 1) // SSM_HEAD_DIM).astype(BF16)
    cw0, cb0 = ssm_conv_w[0], ssm_conv_b[0].reshape(1, conv_dim)
    lambda_init = 0.8 - 0.6 * math.exp(-0.3 * 1)
    lvec = jnp.concatenate([lambda_q1, lambda_k1, lambda_q2, lambda_k2], axis=0)
    sub_w = subln[0].reshape(1, hw)
    scale = (hw // 2) ** -0.5

    g0 = norm_mix[0:1]
    tm_in = _row_tile(seq, 512)
    zx_p, tail_p = in_proj_prompt(xp, g0, w_in, cw0, cb0, seq, d_inner, tm=tm_in)
    gact_p, ssm_state_p = ssd_prompt(zx_p, bp, seq, d_inner, dtb, alog, dskip_hp, ssm_nw, expand)
    xp = matmul_residual(gact_p, w_out, xp, name="out_proj_p")
    ssm_conv_p = tail_p.reshape(bp, seq // tm_in, 8, conv_dim)[:, -1, 8 - (SSM_CONV_W - 1):, :]

    (zx_s,) = norm_matmul(xs, g0, [0], [w_in], [[F32]], tn=768, name="in_proj_s")
    gact_s, newbuf_s, ssm_state_s = ssm_step_sample(
        zx_s, jnp.swapaxes(state_conv_ssm[0], 0, 1), state_ssm[0], d_inner, cw0, cb0, dtb, alog,
        dskip_hp, ssm_nw, expand)
    xs = matmul_residual(gact_s, w_out, xs, name="out_proj_s")
    ssm_conv_s = jnp.swapaxes(newbuf_s, 0, 1)

    ffn_conv_p, ffn_conv_s = [], []

    def conv_ffn(i, xp, xs, final_gain):
        gain = norm_ffn[i:i + 1]
        cw, cb = ffn_conv_w[i], ffn_conv_b[i].reshape(1, f)
        tm = _row_tile(seq, 512)
        h_p, tail = ffn_up_prompt(xp, gain, w_up[i], cw, cb, seq, tm=tm, tn=f // 2, name=f"ffn_up_p{i}")
        xp = matmul_residual(h_p, w_down[i], xp, final_gain, name=f"ffn_down_p{i}")
        tiles_per_seq = seq // tm
        tail = tail.reshape(bp, tiles_per_seq, 8, f)[:, -1, 8 - (FFN_CONV_W - 1):, :]
        ffn_conv_p.append(tail)
        (gu,) = norm_matmul(xs, gain, [0], [w_up[i]], [[F32]], tn=f, name=f"ffn_up_s{i}")
        h_s, nb = ffn_gate_sample(gu, jnp.swapaxes(state_conv_ffn[i], 0, 1), cw, cb)
        xs = matmul_residual(h_s, w_down[i], xs, final_gain, name=f"ffn_down_s{i}")
        ffn_conv_s.append(jnp.swapaxes(nb, 0, 1))
        return xp, xs

    xp, xs = conv_ffn(0, xp, xs, None)

    gains = jnp.concatenate([norm_kv.reshape(1, d), norm_mix[1:2]], axis=0)
    kp, kp16, vp, vp16, qp16 = norm_matmul(
        xp, gains, [0, 0, 1], [wk, wv, wq], [[F32, BF16], [F32, BF16], [BF16]],
        scales=[1.0, 1.0, scale], tm=512, tn=attn_w, name="kvq_p")
    ks, vs, qs16 = norm_matmul(
        xs, gains, [0, 0, 1], [wk, wv, wq], [[F32], [F32], [BF16]],
        scales=[1.0, 1.0, scale], tn=attn_w, name="kvq_s")

    o_p = diff_attention_prompt_flat(qp16.reshape(bp, seq, attn_w), kp16.reshape(bp, seq, attn_w),
                                     vp16.reshape(bp, seq, attn_w), lvec, sub_w, lambda_init)
    xp = matmul_residual(o_p.reshape(mp, attn_w), wo, xp, name="attn_out_p")
    o_s = diff_attention_sample(qs16.reshape(bd, ATTN_HEADS, hw), ks.reshape(bd, ATTN_HEADS, hw),
                                vs.reshape(bd, ATTN_HEADS, hw), cache_k, cache_v, page_table,
                                lvec, sub_w, lambda_init)
    xs = matmul_residual(o_s.reshape(bd, attn_w), wo, xs, name="attn_out_s")

    xp, xs = conv_ffn(1, xp, xs, norm_final.reshape(1, d))

    return (xp.reshape(bp, seq, d), xs.reshape(bd, 1, d),
            ssm_state_p[None], ssm_state_s[None], ssm_conv_p[None], ssm_conv_s[None],
            jnp.stack(ffn_conv_p), jnp.stack(ffn_conv_s),
            kp.reshape(bp, seq, ATTN_HEADS, hw), vp.reshape(bp, seq, ATTN_HEADS, hw),
            ks.reshape(bd, 1, ATTN_HEADS, hw), vs.reshape(bd, 1, ATTN_HEADS, hw))
```

```python
import functools
import math

import jax
import jax.numpy as jnp
from jax import lax
from jax.experimental import pallas as pl
from jax.experimental.pallas import tpu as pltpu

F32 = jnp.float32
BF16 = jnp.bfloat16
EPS = 1e-5
LOG2E = 1.4426950408889634

SSM_HEAD_DIM = 64
SSM_GROUPS = 4
SSM_D_STATE = 128
SSM_CONV_W = 4
SSD_CHUNK = 128
ATTN_HEADS = 8
FFN_CONV_W = 3
PAGES_PER_STEP = 16
DEC_GROUP_PAGES = 2
DEC_SCORE_CHUNK = 16
DEC_PROB_CHUNK = 8

VMEM_LIMIT_BYTES = 56 * 1024 * 1024


def _cparams(*sem):
    return pltpu.CompilerParams(dimension_semantics=sem, vmem_limit_bytes=VMEM_LIMIT_BYTES)


def _silu(x):
    h = 0.5 * x
    return h + h * jnp.tanh(h)


def _rms_unit(x):
    return x * lax.rsqrt(jnp.mean(x * x, axis=-1, keepdims=True) + EPS)


def _split3(a):
    hi = a.astype(BF16)
    r = a - hi.astype(F32)
    mid = r.astype(BF16)
    lo = (r - mid.astype(F32)).astype(BF16)
    return hi, mid, lo


def _dot(a, b):
    return jnp.dot(a, b, preferred_element_type=F32)


def _dot_nt(a, b):
    return lax.dot_general(a, b, (((1,), (1,)), ((), ())), preferred_element_type=F32)


def _dot_exact_rhs(a, b_bf16, parts=3):
    hi, mid, lo = _split3(a)
    out = _dot(hi, b_bf16) + _dot(mid, b_bf16)
    return out + _dot(lo, b_bf16) if parts == 3 else out


def _dot_exact_lhs(a_bf16, b):
    hi, mid, lo = _split3(b)
    return _dot(a_bf16, hi) + _dot(a_bf16, mid) + _dot(a_bf16, lo)


def _row_tile(m, pref):
    t = min(m, pref)
    assert m % t == 0, (m, t)
    return t


def _norm_mm_kernel(n_w, gain_idx, scales, out_dtypes, x_ref, g_ref, *rest):
    w_refs = rest[:n_w]
    o_refs = rest[n_w:-1]
    xn_ref = rest[-1]

    @pl.when(pl.program_id(1) == 0)
    def _():
        xh = _rms_unit(x_ref[...])
        for i in range(n_w):
            gi = gain_idx[i]
            xn_ref[i] = (xh * g_ref[gi:gi + 1, :]).astype(BF16)

    k = 0
    for i in range(n_w):
        r = _dot(xn_ref[i], w_refs[i][...])
        if scales[i] != 1.0:
            r = r * scales[i]
        for dt in out_dtypes[i]:
            o_refs[k][...] = r.astype(dt)
            k += 1


def norm_matmul(x, gains, gain_idx, weights, out_dtypes, scales=None, tm=512, tn=None, name="norm_mm"):
    m, kdim = x.shape
    n = weights[0].shape[1]
    n_w = len(weights)
    scales = tuple(scales) if scales is not None else (1.0,) * n_w
    tm = _row_tile(m, tm)
    tn = n if tn is None else tn
    assert n % tn == 0
    outs, specs = [], []
    for dts in out_dtypes:
        for dt in dts:
            outs.append(jax.ShapeDtypeStruct((m, n), dt))
            specs.append(pl.BlockSpec((tm, tn), lambda i, j: (i, j)))
    kern = functools.partial(_norm_mm_kernel, n_w, tuple(gain_idx), scales,
                             tuple(tuple(d) for d in out_dtypes))
    return pl.pallas_call(
        kern,
        grid=(m // tm, n // tn),
        in_specs=[pl.BlockSpec((tm, kdim), lambda i, j: (i, 0)),
                  pl.BlockSpec(gains.shape, lambda i, j: (0, 0))]
        + [pl.BlockSpec((kdim, tn), lambda i, j: (0, j)) for _ in weights],
        out_specs=specs,
        out_shape=outs,
        scratch_shapes=[pltpu.VMEM((n_w, tm, kdim), BF16)],
        compiler_params=_cparams("parallel", "arbitrary"),
        name=name,
    )(x, gains, *weights)


def _mm_res_kernel(final_norm, a_ref, w_ref, r_ref, *rest):
    o_ref = rest[-1]
    y = r_ref[...] + _dot(a_ref[...], w_ref[...])
    if final_norm:
        y = _rms_unit(y) * rest[0][...]
    o_ref[...] = y


def matmul_residual(a, w, res, final_gain=None, tm=512, name="mm_res"):
    m, kdim = a.shape
    n = w.shape[1]
    tm = _row_tile(m, tm)
    ins = [a, w, res]
    specs = [pl.BlockSpec((tm, kdim), lambda i: (i, 0)),
             pl.BlockSpec((kdim, n), lambda i: (0, 0)),
             pl.BlockSpec((tm, n), lambda i: (i, 0))]
    if final_gain is not None:
        ins.append(final_gain)
        specs.append(pl.BlockSpec((1, n), lambda i: (0, 0)))
    return pl.pallas_call(
        functools.partial(_mm_res_kernel, final_gain is not None),
        grid=(m // tm,),
        in_specs=specs,
        out_specs=pl.BlockSpec((tm, n), lambda i: (i, 0)),
        out_shape=jax.ShapeDtypeStruct((m, n), F32),
        compiler_params=_cparams("parallel"),
        name=name,
    )(*ins)


HALO = 16
IN_PROJ_CHUNK = 256


def _ffn_up_kernel(tm, seq, x_ref, halo_ref, g_ref, wg_ref, wu_ref, cw_ref, cb_ref,
                   h_ref, tail_ref, xn_ref):
    i = pl.program_id(0)

    @pl.when(pl.program_id(1) == 0)
    def _():
        gain = g_ref[...]
        xn_ref[HALO:, :] = (_rms_unit(x_ref[...]) * gain).astype(BF16)
        keep = jnp.where((i * tm) % seq == 0, 0.0, 1.0)
        xn_ref[0:HALO, :] = (_rms_unit(halo_ref[...]) * gain * keep).astype(BF16)

    g = _dot(xn_ref[...], wg_ref[...])
    u = _dot(xn_ref[HALO:, :], wu_ref[...])
    c = (cw_ref[2:3, :] * g + cw_ref[1:2, :] * pltpu.roll(g, 1, 0)
         + cw_ref[0:1, :] * pltpu.roll(g, 2, 0) + cb_ref[...])
    c = c[HALO:, :]
    h_ref[...] = (_silu(c) * u).astype(BF16)
    tail_ref[0] = g[tm + HALO - 8:, :]


def ffn_up_prompt(x, gain, w_up, cw, cb, seq, tm=512, tn=None, name="ffn_up"):
    m, d = x.shape
    f = cw.shape[1]
    tm = _row_tile(seq, tm)
    tn = f if tn is None else tn
    assert f % tn == 0 and tm % HALO == 0
    nj = f // tn
    hb = tm // HALO
    h, tail = pl.pallas_call(
        functools.partial(_ffn_up_kernel, tm, seq),
        grid=(m // tm, nj),
        in_specs=[pl.BlockSpec((tm, d), lambda i, j: (i, 0)),
                  pl.BlockSpec((HALO, d), lambda i, j: (jnp.maximum(i * hb - 1, 0), 0)),
                  pl.BlockSpec((1, d), lambda i, j: (0, 0)),
                  pl.BlockSpec((d, tn), lambda i, j: (0, j)),
                  pl.BlockSpec((d, tn), lambda i, j: (0, nj + j)),
                  pl.BlockSpec((FFN_CONV_W, tn), lambda i, j: (0, j)),
                  pl.BlockSpec((1, tn), lambda i, j: (0, j))],
        out_specs=[pl.BlockSpec((tm, tn), lambda i, j: (i, j)),
                   pl.BlockSpec((1, 8, tn), lambda i, j: (i, 0, j))],
        out_shape=[jax.ShapeDtypeStruct((m, f), BF16),
                   jax.ShapeDtypeStruct((m // tm, 8, f), F32)],
        scratch_shapes=[pltpu.VMEM((HALO + tm, d), BF16)],
        compiler_params=_cparams("parallel", "arbitrary"),
        name=name,
    )(x, x, gain, w_up, w_up, cw, cb)
    return h, tail


def _in_proj_kernel(tm, seq, d_inner, conv_dim, chunk, x_ref, halo_ref, g_ref, w_ref, cw_ref, cb_ref,
                    o_ref, tail_ref, xn_ref):
    i = pl.program_id(0)
    gain = g_ref[...]
    xn_ref[HALO:, :] = (_rms_unit(x_ref[...]) * gain).astype(BF16)
    keep = jnp.where((i * tm) % seq == 0, 0.0, 1.0)
    xn_ref[0:HALO, :] = (_rms_unit(halo_ref[...]) * gain * keep).astype(BF16)

    for c0 in range(0, d_inner, chunk):
        o_ref[:, c0:c0 + chunk] = _silu(_dot(xn_ref[HALO:, :], w_ref[:, c0:c0 + chunk]))
    for j0 in range(0, conv_dim, chunk):
        c0 = d_inner + j0
        g = _dot(xn_ref[...], w_ref[:, c0:c0 + chunk])
        conv = cb_ref[:, j0:j0 + chunk] + cw_ref[SSM_CONV_W - 1:SSM_CONV_W, j0:j0 + chunk] * g
        for k in range(1, SSM_CONV_W):
            conv = conv + cw_ref[SSM_CONV_W - 1 - k:SSM_CONV_W - k, j0:j0 + chunk] * pltpu.roll(g, k, 0)
        o_ref[:, c0:c0 + chunk] = _silu(conv[HALO:, :])
        tail_ref[0, :, j0:j0 + chunk] = g[tm + HALO - 8:, :]
    c0 = d_inner + conv_dim
    o_ref[:, c0:] = _dot(xn_ref[HALO:, :], w_ref[:, c0:])


def in_proj_prompt(x, gain, w, cw, cb, seq, d_inner, tm=512, name="in_proj_p"):
    m, kdim = x.shape
    n = w.shape[1]
    conv_dim = cw.shape[1]
    tm = _row_tile(seq, tm)
    chunk = math.gcd(math.gcd(d_inner, conv_dim), IN_PROJ_CHUNK)
    assert tm % HALO == 0 and chunk % 128 == 0 and n > d_inner + conv_dim
    hb = tm // HALO
    return pl.pallas_call(
        functools.partial(_in_proj_kernel, tm, seq, d_inner, conv_dim, chunk),
        grid=(m // tm,),
        in_specs=[pl.BlockSpec((tm, kdim), lambda i: (i, 0)),
                  pl.BlockSpec((HALO, kdim), lambda i: (jnp.maximum(i * hb - 1, 0), 0)),
                  pl.BlockSpec((1, kdim), lambda i: (0, 0)),
                  pl.BlockSpec((kdim, n), lambda i: (0, 0), pipeline_mode=pl.Buffered(1)),
                  pl.BlockSpec(cw.shape, lambda i: (0, 0)),
                  pl.BlockSpec(cb.shape, lambda i: (0, 0))],
        out_specs=[pl.BlockSpec((tm, n), lambda i: (i, 0)),
                   pl.BlockSpec((1, 8, conv_dim), lambda i: (i, 0, 0))],
        out_shape=[jax.ShapeDtypeStruct((m, n), F32),
                   jax.ShapeDtypeStruct((m // tm, 8, conv_dim), F32)],
        scratch_shapes=[pltpu.VMEM((HALO + tm, kdim), BF16)],
        compiler_params=_cparams("parallel"),
        name=name,
    )(x, x, gain, w, cw, cb)


def _ssd_kernel(d_inner, n_heads, zx_ref, dtb_ref, alog_ref, dskip_ref, nw_ref,
                expand_ref, g_ref, state_ref, ht_ref, y_ref):
    q = SSD_CHUNK
    ns = SSM_D_STATE
    gn = SSM_GROUPS * ns
    conv_dim = d_inner + 2 * gn
    gw = d_inner // SSM_GROUPS
    c = pl.program_id(1)
    last = pl.num_programs(1) - 1

    @pl.when(c == 0)
    def _():
        ht_ref[...] = jnp.zeros(ht_ref.shape, F32)

    xs = zx_ref[:, d_inner:2 * d_inner]
    bm = zx_ref[:, 2 * d_inner:2 * d_inner + gn]
    cm = zx_ref[:, 2 * d_inner + gn:d_inner + conv_dim]

    dt_raw = zx_ref[:, d_inner + conv_dim:d_inner + conv_dim + 128] + dtb_ref[...]
    dt = jnp.maximum(dt_raw, 0.0) + jnp.log1p(jnp.exp(-jnp.abs(dt_raw)))
    da = dt * (-jnp.exp(alog_ref[...]))
    row = lax.broadcasted_iota(jnp.int32, (q, q), 0)
    col = lax.broadcasted_iota(jnp.int32, (q, q), 1)
    tri = row >= col
    a_cum = _dot_exact_lhs(jnp.where(tri, 1.0, 0.0).astype(BF16), da)
    a_cum_t = jnp.transpose(a_cum)

    expand = expand_ref[...]
    dt_hp = _dot_exact_rhs(dt, expand, parts=2)
    a_hp = _dot_exact_rhs(a_cum, expand)
    xdt = xs * dt_hp
    a_last = a_hp[q - 1:q, :]
    xdt_b = xdt.astype(BF16)
    dec_x = (jnp.exp(a_last - a_hp) * xdt).astype(BF16)
    off_scale = jnp.exp(a_hp)
    chunk_decay = jnp.exp(a_last)

    lane = lax.broadcasted_iota(jnp.int32, (q, 2 * SSM_HEAD_DIM), 1)
    heads_per_group = n_heads // SSM_GROUPS
    for g in range(SSM_GROUPS):
        bg = bm[:, g * ns:(g + 1) * ns]
        cg = cm[:, g * ns:(g + 1) * ns].astype(BF16)
        cb_mat = _dot_nt(cg, bg.astype(BF16))
        cols = slice(g * gw, (g + 1) * gw)
        ht = ht_ref[g]
        y_off = _dot(cg, ht.astype(BF16)) * off_scale[:, cols]
        new_states = _dot(jnp.transpose(bg).astype(BF16), dec_x[:, cols])
        ht_ref[g] = ht * chunk_decay[:, cols] + new_states
        for pr in range(heads_per_group // 2):
            c0 = g * gw + pr * 2 * SSM_HEAD_DIM
            x2 = xdt_b[:, c0:c0 + 2 * SSM_HEAD_DIM]
            yd = None
            for half in range(2):
                hh = g * heads_per_group + pr * 2 + half
                seg = a_cum[:, hh:hh + 1] - a_cum_t[hh:hh + 1, :]
                lmat = jnp.where(tri, jnp.exp(jnp.minimum(seg, 0.0)), 0.0)
                mm = (cb_mat * lmat).astype(BF16)
                xh = jnp.where((lane >= SSM_HEAD_DIM) == (half == 1), x2, jnp.zeros_like(x2))
                t = _dot(mm, xh)
                yd = t if yd is None else yd + t
            y_ref[:, c0:c0 + 2 * SSM_HEAD_DIM] = yd + y_off[:, pr * 2 * SSM_HEAD_DIM:(pr + 1) * 2 * SSM_HEAD_DIM]

    y = y_ref[...] + dskip_ref[...] * xs
    gated = y * zx_ref[:, 0:d_inner]
    for g in range(SSM_GROUPS):
        cols = slice(g * gw, (g + 1) * gw)
        g_ref[:, cols] = (_rms_unit(gated[:, cols]) * nw_ref[:, cols]).astype(BF16)

    @pl.when(c == last)
    def _():
        for g in range(SSM_GROUPS):
            st = jnp.transpose(ht_ref[g])
            state_ref[0, g * heads_per_group:(g + 1) * heads_per_group] = st.reshape(
                heads_per_group, SSM_HEAD_DIM, ns)


def ssd_prompt(zx, batch, seq, d_inner, dtb, alog, dskip_hp, nw, expand, name="ssd"):
    m, width = zx.shape
    n_heads = d_inner // SSM_HEAD_DIM
    nc = seq // SSD_CHUNK
    q = SSD_CHUNK
    full = lambda shape: pl.BlockSpec(shape, lambda b, c: (0,) * len(shape))
    return pl.pallas_call(
        functools.partial(_ssd_kernel, d_inner, n_heads),
        grid=(batch, nc),
        in_specs=[pl.BlockSpec((q, width), lambda b, c: (b * nc + c, 0)),
                  full(dtb.shape), full(alog.shape),
                  full(dskip_hp.shape), full(nw.shape), full(expand.shape)],
        out_specs=[pl.BlockSpec((q, d_inner), lambda b, c: (b * nc + c, 0)),
                   pl.BlockSpec((1, n_heads, SSM_HEAD_DIM, SSM_D_STATE), lambda b, c: (b, 0, 0, 0))],
        out_shape=[jax.ShapeDtypeStruct((m, d_inner), BF16),
                   jax.ShapeDtypeStruct((batch, n_heads, SSM_HEAD_DIM, SSM_D_STATE), F32)],
        scratch_shapes=[pltpu.VMEM((SSM_GROUPS, SSM_D_STATE, d_inner // SSM_GROUPS), F32),
                        pltpu.VMEM((q, d_inner), F32)],
        compiler_params=_cparams("parallel", "arbitrary"),
        name=name,
    )(zx, dtb, alog, dskip_hp, nw, expand)


def _diff_lambda(lv, lambda_init):
    s1 = jnp.sum(lv[0:1, :] * lv[1:2, :], axis=-1, keepdims=True)
    s2 = jnp.sum(lv[2:3, :] * lv[3:4, :], axis=-1, keepdims=True)
    return jnp.exp(s1) - jnp.exp(s2) + lambda_init


ATTN_ROW_CHUNK = 64
ATTN_HEADS_PER_STEP = 2


def _attn_flat_kernel(tq, n_hps, nq, lambda_init, q_ref, k_ref, v_ref, lv_ref, sub_ref, o_ref, *scratch):
    hw = q_ref.shape[2] // n_hps
    hd = hw // 2
    rc = ATTN_ROW_CHUNK
    per_head = len(scratch) // n_hps
    heads = []
    for h in range(n_hps):
        r = scratch[h * per_head:(h + 1) * per_head]
        heads.append(dict(q2=r[0:2], f=r[2:4], s=r[4:6], p=r[6], m=r[7], c=r[8], acc=r[9],
                          cols=slice(h * hw, (h + 1) * hw)))
    ones = jnp.ones((tq, hw), BF16)

    def rows_of(tile):
        if isinstance(tile, int):
            return pl.ds(tile * tq, tq)
        return pl.ds(pl.multiple_of(tile * tq, tq), tq)

    def build_q2(hd_, tile, e):
        qh = q_ref[0, rows_of(tile), hd_["cols"]]
        lane = lax.broadcasted_iota(jnp.int32, qh.shape, 1)
        zero = jnp.zeros_like(qh)
        hd_["q2"][e][0:tq, :] = jnp.where(lane < hd, qh, zero)
        hd_["q2"][e][tq:, :] = jnp.where(lane >= hd, qh, zero)

    def init_tile(hd_):
        hd_["m"][...] = jnp.full(hd_["m"].shape, -jnp.inf, F32)
        hd_["acc"][...] = jnp.zeros(hd_["acc"].shape, F32)

    def qk(hd_, e, kb, dst):
        dst[...] = _dot_nt(hd_["q2"][e][...], k_ref[0, rows_of(kb), hd_["cols"]])

    def softmax_pv(hd_, kb, src, masked):
        m_ref, c_ref, p_ref, acc_ref = hd_["m"], hd_["c"], hd_["p"], hd_["acc"]
        for r0 in range(0, 2 * tq, rc):
            rows = slice(r0, r0 + rc)
            s = src[rows, :]
            if masked:
                r = lax.broadcasted_iota(jnp.int32, (rc, tq), 0) + (r0 % tq)
                cidx = lax.broadcasted_iota(jnp.int32, (rc, tq), 1)
                s = jnp.where(r >= cidx, s, -jnp.inf)
            m_old = m_ref[rows, :]
            m_new = jnp.maximum(m_old, jnp.max(s, axis=-1, keepdims=True))
            m_ref[rows, :] = m_new
            c_ref[rows, :] = jnp.exp(m_old - m_new)
            p_ref[rows, :] = jnp.exp(s - m_new).astype(BF16)
        v_ext = jnp.concatenate([v_ref[0, rows_of(kb), hd_["cols"]], ones], axis=1)
        pv = _dot(p_ref[...], v_ext)
        for r0 in range(0, 2 * tq, rc):
            rows = slice(r0, r0 + rc)
            acc_ref[rows, :] = acc_ref[rows, :] * c_ref[rows, :] + pv[rows, :]

    lam = _diff_lambda(lv_ref[...], lambda_init)

    def finalize(hd_, tile):
        acc = hd_["acc"][...]
        o = acc[:, :hw] / acc[:, hw:]
        o = o[:tq] - lam * o[tq:]
        o = _rms_unit(o) * sub_ref[...]
        o_ref[0, rows_of(tile), hd_["cols"]] = (o * (1.0 - lambda_init)).astype(BF16)

    def each(fn, *args):
        for hd_ in heads:
            fn(hd_, *args)

    def start_next(tile, e):
        nxt = jnp.minimum(tile + 1, nq - 1)
        each(build_q2, nxt, 1 - e)
        for hd_ in heads:
            qk(hd_, 1 - e, 0, hd_["f"][1 - e])

    def tile_body(tile, e, only_diagonal):
        each(init_tile)
        if only_diagonal:
            start_next(tile, e)
            for hd_ in heads:
                softmax_pv(hd_, 0, hd_["f"][e], True)
        else:
            for hd_ in heads:
                qk(hd_, e, 1, hd_["s"][0])
            for hd_ in heads:
                softmax_pv(hd_, 0, hd_["f"][e], False)

            def body(kk, carry):
                kb = 1 + 2 * kk
                for hd_ in heads:
                    qk(hd_, e, kb + 1, hd_["s"][1])
                for hd_ in heads:
                    softmax_pv(hd_, kb, hd_["s"][0], False)
                for hd_ in heads:
                    qk(hd_, e, kb + 2, hd_["s"][0])
                for hd_ in heads:
                    softmax_pv(hd_, kb + 1, hd_["s"][1], False)
                return carry

            lax.fori_loop(0, (tile - 1) // 2, body, 0)
            if e == 0:
                for hd_ in heads:
                    qk(hd_, e, tile, hd_["s"][1])
                for hd_ in heads:
                    softmax_pv(hd_, tile - 1, hd_["s"][0], False)
                start_next(tile, e)
                for hd_ in heads:
                    softmax_pv(hd_, tile, hd_["s"][1], True)
            else:
                start_next(tile, e)
                for hd_ in heads:
                    softmax_pv(hd_, tile, hd_["s"][0], True)
        each(finalize, tile)

    each(build_q2, 0, 0)
    for hd_ in heads:
        qk(hd_, 0, 0, hd_["f"][0])

    def pair_body(t, carry):
        @pl.when(t == 0)
        def _():
            tile_body(0, 0, True)

        @pl.when(t > 0)
        def _():
            tile_body(2 * t, 0, False)

        tile_body(2 * t + 1, 1, False)
        return carry

    lax.fori_loop(0, nq // 2, pair_body, 0)


def diff_attention_prompt_flat(q, k, v, lvec, subln, lambda_init, tq=512, name="diff_attn"):
    b, l, width = q.shape
    hw = width // ATTN_HEADS
    n_hps = ATTN_HEADS_PER_STEP if ATTN_HEADS % ATTN_HEADS_PER_STEP == 0 else 1
    bw = n_hps * hw
    tq = _row_tile(l, tq)
    nq = l // tq
    assert tq % ATTN_ROW_CHUNK == 0 and nq % 2 == 0
    head_scratch = ([pltpu.VMEM((2 * tq, hw), BF16)] * 2
                    + [pltpu.VMEM((2 * tq, tq), F32)] * 4
                    + [pltpu.VMEM((2 * tq, tq), BF16),
                       pltpu.VMEM((2 * tq, 1), F32),
                       pltpu.VMEM((2 * tq, 1), F32),
                       pltpu.VMEM((2 * tq, 2 * hw), F32)])
    seq_spec = pl.BlockSpec((1, l, bw), lambda bi, h: (bi, 0, h))
    return pl.pallas_call(
        functools.partial(_attn_flat_kernel, tq, n_hps, nq, lambda_init),
        grid=(b, ATTN_HEADS // n_hps),
        in_specs=[seq_spec, seq_spec, seq_spec,
                  pl.BlockSpec(lvec.shape, lambda bi, h: (0, 0)),
                  pl.BlockSpec(subln.shape, lambda bi, h: (0, 0))],
        out_specs=seq_spec,
        out_shape=jax.ShapeDtypeStruct((b, l, width), BF16),
        scratch_shapes=head_scratch * n_hps,
        compiler_params=_cparams("parallel", "parallel"),
        name=name,
    )(q, k, v, lvec, subln)


def _ssm_prep_kernel(d_inner, zx_ref, buf_ref, cw_ref, cb_ref, dtb_ref, alog_ref, expand_ref,
                     newbuf_ref, xs_ref, xdt_ref, b_ref, c_ref, decay_ref):
    gn = SSM_GROUPS * SSM_D_STATE
    conv_dim = d_inner + 2 * gn
    xnew = zx_ref[:, d_inner:d_inner + conv_dim]
    conv = cb_ref[...] + cw_ref[SSM_CONV_W - 1:SSM_CONV_W, :] * xnew
    for k in range(SSM_CONV_W - 1):
        conv = conv + cw_ref[k:k + 1, :] * buf_ref[k]
    for k in range(SSM_CONV_W - 2):
        newbuf_ref[k] = buf_ref[k + 1]
    newbuf_ref[SSM_CONV_W - 2] = xnew
    xbc = _silu(conv)
    xs = xbc[:, :d_inner]
    dt_raw = zx_ref[:, d_inner + conv_dim:d_inner + conv_dim + 128] + dtb_ref[...]
    dt = jnp.maximum(dt_raw, 0.0) + jnp.log1p(jnp.exp(-jnp.abs(dt_raw)))
    decay_ref[...] = jnp.exp(dt * (-jnp.exp(alog_ref[...])))
    xs_ref[...] = xs
    xdt_ref[...] = xs * _dot_exact_rhs(dt, expand_ref[...])
    b_ref[...] = xbc[:, d_inner:d_inner + gn]
    c_ref[...] = xbc[:, d_inner + gn:]


def _col_bcast(rowvec):
    n = rowvec.shape[1]
    wide = jnp.broadcast_to(rowvec, (128, n))
    return jnp.concatenate([jnp.transpose(wide[:, j * 128:(j + 1) * 128]) for j in range(n // 128)], axis=0)


def _ssm_state_kernel(n_heads, h0_ref, xdt_ref, decay_ref, b_ref, c_ref, hn_ref, y_ref):
    ns = SSM_D_STATE
    hpg = n_heads // SSM_GROUPS
    xcol = _col_bcast(xdt_ref[0])
    dcol = _col_bcast(decay_ref[0])
    for g in range(SSM_GROUPS):
        brow = b_ref[0, :, g * ns:(g + 1) * ns]
        crow = jnp.broadcast_to(c_ref[0, :, g * ns:(g + 1) * ns], (8, ns))
        chi, cmid, _ = _split3(crow)
        parts = []
        for r in range(hpg):
            h = g * hpg + r
            hn = (h0_ref[0, h] * dcol[h:h + 1, :]
                  + xcol[h * SSM_HEAD_DIM:(h + 1) * SSM_HEAD_DIM, :] * brow)
            hn_ref[0, h] = hn
            parts.append(hn)
        hhi, hmid, _ = _split3(jnp.concatenate(parts, axis=0))
        yr = _dot_nt(chi, hhi) + _dot_nt(cmid, hhi) + _dot_nt(chi, hmid)
        y_ref[0, :, g * hpg * SSM_HEAD_DIM:(g + 1) * hpg * SSM_HEAD_DIM] = yr[0:1, :]


def _ssm_post_kernel(d_inner, y_ref, xs_ref, zx_ref, dskip_ref, nw_ref, g_ref):
    gw = d_inner // SSM_GROUPS
    y = y_ref[...] + dskip_ref[...] * xs_ref[...]
    gated = y * _silu(zx_ref[:, 0:d_inner])
    for g in range(SSM_GROUPS):
        cols = slice(g * gw, (g + 1) * gw)
        g_ref[:, cols] = (_rms_unit(gated[:, cols]) * nw_ref[:, cols]).astype(BF16)


def ssm_step_sample(zx, conv_buf_t, h0, d_inner, cw, cb, dtb, alog, dskip_hp, nw, expand):
    bsz = zx.shape[0]
    n_heads = d_inner // SSM_HEAD_DIM
    gn = SSM_GROUPS * SSM_D_STATE
    conv_dim = d_inner + 2 * gn
    newbuf, xs, xdt, bm, cm, decay = pl.pallas_call(
        functools.partial(_ssm_prep_kernel, d_inner),
        out_shape=[jax.ShapeDtypeStruct(conv_buf_t.shape, F32),
                   jax.ShapeDtypeStruct((bsz, d_inner), F32),
                   jax.ShapeDtypeStruct((bsz, d_inner), F32),
                   jax.ShapeDtypeStruct((bsz, gn), F32),
                   jax.ShapeDtypeStruct((bsz, gn), F32),
                   jax.ShapeDtypeStruct((bsz, 128), F32)],
        compiler_params=pltpu.CompilerParams(vmem_limit_bytes=VMEM_LIMIT_BYTES),
        name="ssm_prep",
    )(zx, conv_buf_t, cw, cb, dtb, alog, expand)
    row3 = lambda a: a.reshape(bsz, 1, a.shape[1])
    hn, y = pl.pallas_call(
        functools.partial(_ssm_state_kernel, n_heads),
        grid=(bsz,),
        in_specs=[pl.BlockSpec((1, n_heads, SSM_HEAD_DIM, SSM_D_STATE), lambda b: (b, 0, 0, 0)),
                  pl.BlockSpec((1, 1, d_inner), lambda b: (b, 0, 0)),
                  pl.BlockSpec((1, 1, 128), lambda b: (b, 0, 0)),
                  pl.BlockSpec((1, 1, gn), lambda b: (b, 0, 0)),
                  pl.BlockSpec((1, 1, gn), lambda b: (b, 0, 0))],
        out_specs=[pl.BlockSpec((1, n_heads, SSM_HEAD_DIM, SSM_D_STATE), lambda b: (b, 0, 0, 0)),
                   pl.BlockSpec((1, 1, d_inner), lambda b: (b, 0, 0))],
        out_shape=[jax.ShapeDtypeStruct(h0.shape, F32),
                   jax.ShapeDtypeStruct((bsz, 1, d_inner), F32)],
        compiler_params=_cparams("parallel"),
        name="ssm_state",
    )(h0, row3(xdt), row3(decay), row3(bm), row3(cm))
    gact = pl.pallas_call(
        functools.partial(_ssm_post_kernel, d_inner),
        out_shape=jax.ShapeDtypeStruct((bsz, d_inner), BF16),
        compiler_params=pltpu.CompilerParams(vmem_limit_bytes=VMEM_LIMIT_BYTES),
        name="ssm_post",
    )(y.reshape(bsz, d_inner), xs, zx, dskip_hp, nw)
    return gact, newbuf, hn


def _ffn_gate_sample_kernel(f, gu_ref, buf_ref, cw_ref, cb_ref, h_ref, newbuf_ref):
    g = gu_ref[:, 0:f]
    u = gu_ref[:, f:2 * f]
    c = cb_ref[...] + cw_ref[FFN_CONV_W - 1:FFN_CONV_W, :] * g
    for k in range(FFN_CONV_W - 1):
        c = c + cw_ref[k:k + 1, :] * buf_ref[k]
    for k in range(FFN_CONV_W - 2):
        newbuf_ref[k] = buf_ref[k + 1]
    newbuf_ref[FFN_CONV_W - 2] = g
    h_ref[...] = (_silu(c) * u).astype(BF16)


def ffn_gate_sample(gu, buf_t, cw, cb):
    bsz = gu.shape[0]
    f = cw.shape[1]
    return pl.pallas_call(
        functools.partial(_ffn_gate_sample_kernel, f),
        out_shape=[jax.ShapeDtypeStruct((bsz, f), BF16),
                   jax.ShapeDtypeStruct(buf_t.shape, F32)],
        compiler_params=pltpu.CompilerParams(vmem_limit_bytes=VMEM_LIMIT_BYTES),
        name="ffn_gate_sample",
    )(gu, buf_t, cw, cb)


def _dec_attn_kernel(npg, lambda_init, pt_ref, q_ref, kn_ref, vn_ref, lv_ref, sub_ref, *rest):
    k_refs = rest[:npg]
    v_refs = rest[npg:2 * npg]
    o_ref = rest[2 * npg]
    s_ref, m_ref, l_ref, acc_ref = rest[2 * npg + 1:]
    del pt_ref
    j = pl.program_id(1)
    page, nh, hw = k_refs[0].shape[1:]
    hd = hw // 2
    sc, pc = DEC_SCORE_CHUNK, DEC_PROB_CHUNK

    qv = q_ref[0].astype(F32) * LOG2E
    srow = lax.broadcasted_iota(jnp.int32, (hw, hw), 0)
    scol = lax.broadcasted_iota(jnp.int32, (hw, hw), 1)
    seg = jnp.where(srow // hd == scol // hd, 1.0, 0.0).astype(BF16)

    def scores(k_tok):
        n_tok = k_tok.shape[0]
        prod = (k_tok * qv[None]).reshape(n_tok * nh, hw).astype(BF16)
        return _dot(prod, seg).reshape(n_tok, nh, hw)

    def swap_halves(x):
        return pltpu.roll(x.reshape(-1, hw), hd, 1).reshape(x.shape)

    @pl.when(j == 0)
    def _():
        m_ref[...] = scores(kn_ref[...])[0]
        l_ref[...] = jnp.ones(l_ref.shape, F32)
        vn = vn_ref[0]
        acc_ref[...] = jnp.concatenate([vn, vn], axis=1)

    gp = DEC_GROUP_PAGES
    n_groups = npg // gp

    def score_chunks(g):
        return [(p, t0) for p in range(g * gp, (g + 1) * gp) for t0 in range(0, page, sc)]

    def prob_chunks(g):
        return [(p, t0) for p in range(g * gp, (g + 1) * gp) for t0 in range(0, page, pc)]

    def score_chunk(p, t0, m_run):
        s = scores(k_refs[p][0, t0:t0 + sc])
        s_ref[p, t0:t0 + sc] = s
        return jnp.maximum(m_run, jnp.max(s, axis=0))

    m_cur = m_ref[...]
    l_run = l_ref[...]
    acc = acc_ref[...]
    acc_a, acc_b = acc[:, :hw], acc[:, hw:]
    m_next = m_cur
    for p, t0 in score_chunks(0):
        m_next = score_chunk(p, t0, m_next)
    for g in range(n_groups):
        m_new = m_next
        corr = jnp.exp2(m_cur - m_new)
        l_run = l_run * corr
        acc_a = acc_a * corr
        acc_b = acc_b * swap_halves(corr)
        m_cur = m_new
        pending = score_chunks(g + 1) if g + 1 < n_groups else []
        every = sc // pc
        for i, (p, t0) in enumerate(prob_chunks(g)):
            if pending and i % every == 0:
                m_next = score_chunk(*pending.pop(0), m_next)
            pr = jnp.exp2(s_ref[p, t0:t0 + pc] - m_new[None])
            l_run = l_run + jnp.sum(pr, axis=0)
            vp = v_refs[p][0, t0:t0 + pc]
            acc_a = acc_a + jnp.sum(pr * vp, axis=0)
            acc_b = acc_b + jnp.sum(swap_halves(pr) * vp, axis=0)
        assert not pending
    m_ref[...] = m_cur
    l_ref[...] = l_run
    acc_ref[...] = jnp.concatenate([acc_a, acc_b], axis=1)

    @pl.when(j == pl.num_programs(1) - 1)
    def _():
        xa = acc_a / l_run
        xb = acc_b / swap_halves(l_run)
        lam = _diff_lambda(lv_ref[...], lambda_init)
        low = lax.broadcasted_iota(jnp.int32, (nh, hw), 1) < hd
        o = jnp.where(low, xa - lam * xb, xb - lam * xa)
        o = _rms_unit(o) * sub_ref[...]
        o_ref[0] = (o * (1.0 - lambda_init)).astype(BF16)


def diff_attention_sample(q, k_new, v_new, cache_k, cache_v, page_table, lvec, subln, lambda_init):
    bsz, nh, hw = q.shape
    n_pages = page_table.shape[1]
    page = cache_k.shape[1]
    npg = min(PAGES_PER_STEP, n_pages)
    assert n_pages % npg == 0
    row = pl.BlockSpec((1, nh, hw), lambda b, j, pt: (b, 0, 0))

    def page_spec(p):
        return pl.BlockSpec((1, page, nh, hw), lambda b, j, pt: (pt[b, j * npg + p], 0, 0, 0))

    grid_spec = pltpu.PrefetchScalarGridSpec(
        num_scalar_prefetch=1,
        grid=(bsz, n_pages // npg),
        in_specs=[row, row, row,
                  pl.BlockSpec(lvec.shape, lambda b, j, pt: (0, 0)),
                  pl.BlockSpec(subln.shape, lambda b, j, pt: (0, 0))]
        + [page_spec(p) for p in range(npg)] + [page_spec(p) for p in range(npg)],
        out_specs=row,
        scratch_shapes=[pltpu.VMEM((npg, page, nh, hw), F32),
                        pltpu.VMEM((nh, hw), F32),
                        pltpu.VMEM((nh, hw), F32),
                        pltpu.VMEM((nh, 2 * hw), F32)],
    )
    return pl.pallas_call(
        functools.partial(_dec_attn_kernel, npg, lambda_init),
        grid_spec=grid_spec,
        out_shape=jax.ShapeDtypeStruct((bsz, nh, hw), BF16),
        compiler_params=_cparams("parallel", "arbitrary"),
        name="diff_attn_sample",
    )(page_table, q, k_new, v_new, lvec, subln, *([cache_k] * npg), *([cache_v] * npg))


def kernel(x_prompt, x_sample, state_ssm, state_conv_ssm, state_conv_ffn, cache_k, cache_v, page_table,
           norm_mix, norm_ffn, ssm_in_proj, ssm_conv_w, ssm_conv_b, ssm_dt_bias, ssm_a_log, ssm_d,
           ssm_norm, ssm_out_proj, ffn_up, ffn_conv_w, ffn_conv_b, ffn_down, norm_kv, w_k, w_v,
           w_q, lambda_q1, lambda_k1, lambda_q2, lambda_k2, subln, w_o, norm_final):
    bp, seq, d = x_prompt.shape
    bd = x_sample.shape[0]
    depth = norm_mix.shape[0]
    assert depth == 2 and ssm_in_proj.shape[0] == 1 and x_sample.shape[1] == 1
    d_inner = ssm_norm.shape[1]
    n_heads = ssm_dt_bias.shape[1]
    assert n_heads * SSM_HEAD_DIM == d_inner and n_heads <= 128
    conv_dim = ssm_conv_w.shape[2]
    f = ffn_conv_w.shape[2]
    attn_w = w_k.shape[1]
    hw = attn_w // ATTN_HEADS
    mp = bp * seq

    xp = x_prompt.reshape(mp, d)
    xs = x_sample.reshape(bd, d)

    in_dim = ssm_in_proj.shape[2]
    in_pad = -(-(d_inner + conv_dim + 128) // 768) * 768
    w_in = jnp.pad(ssm_in_proj[0], ((0, 0), (0, in_pad - in_dim))).astype(BF16)
    w_out = ssm_out_proj[0].astype(BF16)
    w_up_all, w_down_all = ffn_up.astype(BF16), ffn_down.astype(BF16)
    w_up = [w_up_all[i] for i in range(depth)]
    w_down = [w_down_all[i] for i in range(depth)]
    wk, wv, wq, wo = w_k.astype(BF16), w_v.astype(BF16), w_q[0].astype(BF16), w_o[0].astype(BF16)
    dtb = jnp.pad(ssm_dt_bias[0], (0, 128 - n_heads)).reshape(1, 128)
    alog = jnp.pad(ssm_a_log[0], (0, 128 - n_heads)).reshape(1, 128)
    dskip_hp = jnp.repeat(ssm_d[0], SSM_HEAD_DIM).reshape(1, d_inner)
    ssm_nw = ssm_norm[0].reshape(1, d_inner)
    expand = (lax.broadcasted_iota(jnp.int32, (128, d_inner), 0)
              == lax.broadcasted_iota(jnp.int32, (128, d_inner), 1) // SSM_HEAD_DIM).astype(BF16)
    cw0, cb0 = ssm_conv_w[0], ssm_conv_b[0].reshape(1, conv_dim)
    lambda_init = 0.8 - 0.6 * math.exp(-0.3 * 1)
    lvec = jnp.concatenate([lambda_q1, lambda_k1, lambda_q2, lambda_k2], axis=0)
    sub_w = subln[0].reshape(1, hw)
    scale = (hw // 2) ** -0.5

    g0 = norm_mix[0:1]
    tm_in = _row_tile(seq, 512)
    zx_p, tail_p = in_proj_prompt(xp, g0, w_in, cw0, cb0, seq, d_inner, tm=tm_in)
    gact_p, ssm_state_p = ssd_prompt(zx_p, bp, seq, d_inner, dtb, alog, dskip_hp, ssm_nw, expand)
    xp = matmul_residual(gact_p, w_out, xp, name="out_proj_p")
    ssm_conv_p = tail_p.reshape(bp, seq // tm_in, 8, conv_dim)[:, -1, 8 - (SSM_CONV_W - 1):, :]

    (zx_s,) = norm_matmul(xs, g0, [0], [w_in], [[F32]], tn=768, name="in_proj_s")
    gact_s, newbuf_s, ssm_state_s = ssm_step_sample(
        zx_s, jnp.swapaxes(state_conv_ssm[0], 0, 1), state_ssm[0], d_inner, cw0, cb0, dtb, alog,
        dskip_hp, ssm_nw, expand)
    xs = matmul_residual(gact_s, w_out, xs, name="out_proj_s")
    ssm_conv_s = jnp.swapaxes(newbuf_s, 0, 1)

    ffn_conv_p, ffn_conv_s = [], []

    def conv_ffn(i, xp, xs, final_gain):
        gain = norm_ffn[i:i + 1]
        cw, cb = ffn_conv_w[i], ffn_conv_b[i].reshape(1, f)
        tm = _row_tile(seq, 512)
        h_p, tail = ffn_up_prompt(xp, gain, w_up[i], cw, cb, seq, tm=tm, tn=f // 2, name=f"ffn_up_p{i}")
        xp = matmul_residual(h_p, w_down[i], xp, final_gain, name=f"ffn_down_p{i}")
        tiles_per_seq = seq // tm
        tail = tail.reshape(bp, tiles_per_seq, 8, f)[:, -1, 8 - (FFN_CONV_W - 1):, :]
        ffn_conv_p.append(tail)
        (gu,) = norm_matmul(xs, gain, [0], [w_up[i]], [[F32]], tn=f, name=f"ffn_up_s{i}")
        h_s, nb = ffn_gate_sample(gu, jnp.swapaxes(state_conv_ffn[i], 0, 1), cw, cb)
        xs = matmul_residual(h_s, w_down[i], xs, final_gain, name=f"ffn_down_s{i}")
        ffn_conv_s.append(jnp.swapaxes(nb, 0, 1))
        return xp, xs

    xp, xs = conv_ffn(0, xp, xs, None)

    gains = jnp.concatenate([norm_kv.reshape(1, d), norm_mix[1:2]], axis=0)
    kp, kp16, vp, vp16, qp16 = norm_matmul(
        xp, gains, [0, 0, 1], [wk, wv, wq], [[F32, BF16], [F32, BF16], [BF16]],
        scales=[1.0, 1.0, scale], tm=512, tn=attn_w, name="kvq_p")
    ks, vs, qs16 = norm_matmul(
        xs, gains, [0, 0, 1], [wk, wv, wq], [[F32], [F32], [BF16]],
        scales=[1.0, 1.0, scale], tn=attn_w, name="kvq_s")

    o_p = diff_attention_prompt_flat(qp16.reshape(bp, seq, attn_w), kp16.reshape(bp, seq, attn_w),
                                     vp16.reshape(bp, seq, attn_w), lvec, sub_w, lambda_init)
    xp = matmul_residual(o_p.reshape(mp, attn_w), wo, xp, name="attn_out_p")
    o_s = diff_attention_sample(qs16.reshape(bd, ATTN_HEADS, hw), ks.reshape(bd, ATTN_HEADS, hw),
                                vs.reshape(bd, ATTN_HEADS, hw), cache_k, cache_v, page_table,
                                lvec, sub_w, lambda_init)
    xs = matmul_residual(o_s.reshape(bd, attn_w), wo, xs, name="attn_out_s")

    xp, xs = conv_ffn(1, xp, xs, norm_final.reshape(1, d))

    return (xp.reshape(bp, seq, d), xs.reshape(bd, 1, d),
            ssm_state_p[None], ssm_state_s[None], ssm_conv_p[None], ssm_conv_s[None],
            jnp.stack(ffn_conv_p), jnp.stack(ffn_conv_s),
            kp.reshape(bp, seq, ATTN_HEADS, hw), vp.reshape(bp, seq, ATTN_HEADS, hw),
            ks.reshape(bd, 1, ATTN_HEADS, hw), vs.reshape(bd, 1, ATTN_HEADS, hw))
```

```python
import functools
import math

import jax
import jax.numpy as jnp
from jax import lax
from jax.experimental import pallas as pl
from jax.experimental.pallas import tpu as pltpu

F32 = jnp.float32
BF16 = jnp.bfloat16
EPS = 1e-5
LOG2E = 1.4426950408889634

SSM_HEAD_DIM = 64
SSM_GROUPS = 4
SSM_D_STATE = 128
SSM_CONV_W = 4
SSD_CHUNK = 128
ATTN_HEADS = 8
FFN_CONV_W = 3
PAGES_PER_STEP = 16
DEC_GROUP_PAGES = 2
DEC_SCORE_CHUNK = 16
DEC_PROB_CHUNK = 8

VMEM_LIMIT_BYTES = 56 * 1024 * 1024


def _cparams(*sem):
    return pltpu.CompilerParams(dimension_semantics=sem, vmem_limit_bytes=VMEM_LIMIT_BYTES)


def _silu(x):
    h = 0.5 * x
    return h + h * jnp.tanh(h)


def _rms_unit(x):
    return x * lax.rsqrt(jnp.mean(x * x, axis=-1, keepdims=True) + EPS)


def _split3(a):
    hi = a.astype(BF16)
    r = a - hi.astype(F32)
    mid = r.astype(BF16)
    lo = (r - mid.astype(F32)).astype(BF16)
    return hi, mid, lo


def _dot(a, b):
    return jnp.dot(a, b, preferred_element_type=F32)


def _dot_nt(a, b):
    return lax.dot_general(a, b, (((1,), (1,)), ((), ())), preferred_element_type=F32)


def _dot_exact_rhs(a, b_bf16, parts=3):
    hi, mid, lo = _split3(a)
    out = _dot(hi, b_bf16) + _dot(mid, b_bf16)
    return out + _dot(lo, b_bf16) if parts == 3 else out


def _dot_exact_lhs(a_bf16, b):
    hi, mid, lo = _split3(b)
    return _dot(a_bf16, hi) + _dot(a_bf16, mid) + _dot(a_bf16, lo)


def _row_tile(m, pref):
    t = min(m, pref)
    assert m % t == 0, (m, t)
    return t


def _norm_mm_kernel(n_w, gain_idx, scales, out_dtypes, x_ref, g_ref, *rest):
    w_refs = rest[:n_w]
    o_refs = rest[n_w:-1]
    xn_ref = rest[-1]

    @pl.when(pl.program_id(1) == 0)
    def _():
        xh = _rms_unit(x_ref[...])
        for i in range(n_w):
            gi = gain_idx[i]
            xn_ref[i] = (xh * g_ref[gi:gi + 1, :]).astype(BF16)

    k = 0
    for i in range(n_w):
        r = _dot(xn_ref[i], w_refs[i][...])
        if scales[i] != 1.0:
            r = r * scales[i]
        for dt in out_dtypes[i]:
            o_refs[k][...] = r.astype(dt)
            k += 1


def norm_matmul(x, gains, gain_idx, weights, out_dtypes, scales=None, tm=512, tn=None, name="norm_mm"):
    m, kdim = x.shape
    n = weights[0].shape[1]
    n_w = len(weights)
    scales = tuple(scales) if scales is not None else (1.0,) * n_w
    tm = _row_tile(m, tm)
    tn = n if tn is None else tn
    assert n % tn == 0
    outs, specs = [], []
    for dts in out_dtypes:
        for dt in dts:
            outs.append(jax.ShapeDtypeStruct((m, n), dt))
            specs.append(pl.BlockSpec((tm, tn), lambda i, j: (i, j)))
    kern = functools.partial(_norm_mm_kernel, n_w, tuple(gain_idx), scales,
                             tuple(tuple(d) for d in out_dtypes))
    return pl.pallas_call(
        kern,
        grid=(m // tm, n // tn),
        in_specs=[pl.BlockSpec((tm, kdim), lambda i, j: (i, 0)),
                  pl.BlockSpec(gains.shape, lambda i, j: (0, 0))]
        + [pl.BlockSpec((kdim, tn), lambda i, j: (0, j)) for _ in weights],
        out_specs=specs,
        out_shape=outs,
        scratch_shapes=[pltpu.VMEM((n_w, tm, kdim), BF16)],
        compiler_params=_cparams("parallel", "arbitrary"),
        name=name,
    )(x, gains, *weights)


def _mm_res_kernel(final_norm, a_ref, w_ref, r_ref, *rest):
    o_ref = rest[-1]
    y = r_ref[...] + _dot(a_ref[...], w_ref[...])
    if final_norm:
        y = _rms_unit(y) * rest[0][...]
    o_ref[...] = y


def matmul_residual(a, w, res, final_gain=None, tm=512, name="mm_res"):
    m, kdim = a.shape
    n = w.shape[1]
    tm = _row_tile(m, tm)
    ins = [a, w, res]
    specs = [pl.BlockSpec((tm, kdim), lambda i: (i, 0)),
             pl.BlockSpec((kdim, n), lambda i: (0, 0)),
             pl.BlockSpec((tm, n), lambda i: (i, 0))]
    if final_gain is not None:
        ins.append(final_gain)
        specs.append(pl.BlockSpec((1, n), lambda i: (0, 0)))
    return pl.pallas_call(
        functools.partial(_mm_res_kernel, final_gain is not None),
        grid=(m // tm,),
        in_specs=specs,
        out_specs=pl.BlockSpec((tm, n), lambda i: (i, 0)),
        out_shape=jax.ShapeDtypeStruct((m, n), F32),
        compiler_params=_cparams("parallel"),
        name=name,
    )(*ins)


HALO = 16
IN_PROJ_CHUNK = 256


def _ffn_up_kernel(tm, seq, x_ref, halo_ref, g_ref, wg_ref, wu_ref, cw_ref, cb_ref,
                   h_ref, tail_ref, xn_ref):
    i = pl.program_id(0)

    @pl.when(pl.program_id(1) == 0)
    def _():
        gain = g_ref[...]
        xn_ref[HALO:, :] = (_rms_unit(x_ref[...]) * gain).astype(BF16)
        keep = jnp.where((i * tm) % seq == 0, 0.0, 1.0)
        xn_ref[0:HALO, :] = (_rms_unit(halo_ref[...]) * gain * keep).astype(BF16)

    g = _dot(xn_ref[...], wg_ref[...])
    u = _dot(xn_ref[HALO:, :], wu_ref[...])
    c = (cw_ref[2:3, :] * g + cw_ref[1:2, :] * pltpu.roll(g, 1, 0)
         + cw_ref[0:1, :] * pltpu.roll(g, 2, 0) + cb_ref[...])
    c = c[HALO:, :]
    h_ref[...] = (_silu(c) * u).astype(BF16)
    tail_ref[0] = g[tm + HALO - 8:, :]


def ffn_up_prompt(x, gain, w_up, cw, cb, seq, tm=512, tn=None, name="ffn_up"):
    m, d = x.shape
    f = cw.shape[1]
    tm = _row_tile(seq, tm)
    tn = f if tn is None else tn
    assert f % tn == 0 and tm % HALO == 0
    nj = f // tn
    hb = tm // HALO
    h, tail = pl.pallas_call(
        functools.partial(_ffn_up_kernel, tm, seq),
        grid=(m // tm, nj),
        in_specs=[pl.BlockSpec((tm, d), lambda i, j: (i, 0)),
                  pl.BlockSpec((HALO, d), lambda i, j: (jnp.maximum(i * hb - 1, 0), 0)),
                  pl.BlockSpec((1, d), lambda i, j: (0, 0)),
                  pl.BlockSpec((d, tn), lambda i, j: (0, j)),
                  pl.BlockSpec((d, tn), lambda i, j: (0, nj + j)),
                  pl.BlockSpec((FFN_CONV_W, tn), lambda i, j: (0, j)),
                  pl.BlockSpec((1, tn), lambda i, j: (0, j))],
        out_specs=[pl.BlockSpec((tm, tn), lambda i, j: (i, j)),
                   pl.BlockSpec((1, 8, tn), lambda i, j: (i, 0, j))],
        out_shape=[jax.ShapeDtypeStruct((m, f), BF16),
                   jax.ShapeDtypeStruct((m // tm, 8, f), F32)],
        scratch_shapes=[pltpu.VMEM((HALO + tm, d), BF16)],
        compiler_params=_cparams("parallel", "arbitrary"),
        name=name,
    )(x, x, gain, w_up, w_up, cw, cb)
    return h, tail


def _in_proj_kernel(tm, seq, d_inner, conv_dim, chunk, x_ref, halo_ref, g_ref, w_ref, cw_ref, cb_ref,
                    o_ref, tail_ref, xn_ref):
    i = pl.program_id(0)
    gain = g_ref[...]
    xn_ref[HALO:, :] = (_rms_unit(x_ref[...]) * gain).astype(BF16)
    keep = jnp.where((i * tm) % seq == 0, 0.0, 1.0)
    xn_ref[0:HALO, :] = (_rms_unit(halo_ref[...]) * gain * keep).astype(BF16)

    for c0 in range(0, d_inner, chunk):
        o_ref[:, c0:c0 + chunk] = _silu(_dot(xn_ref[HALO:, :], w_ref[:, c0:c0 + chunk]))
    for j0 in range(0, conv_dim, chunk):
        c0 = d_inner + j0
        g = _dot(xn_ref[...], w_ref[:, c0:c0 + chunk])
        conv = cb_ref[:, j0:j0 + chunk] + cw_ref[SSM_CONV_W - 1:SSM_CONV_W, j0:j0 + chunk] * g
        for k in range(1, SSM_CONV_W):
            conv = conv + cw_ref[SSM_CONV_W - 1 - k:SSM_CONV_W - k, j0:j0 + chunk] * pltpu.roll(g, k, 0)
        o_ref[:, c0:c0 + chunk] = _silu(conv[HALO:, :])
        tail_ref[0, :, j0:j0 + chunk] = g[tm + HALO - 8:, :]
    c0 = d_inner + conv_dim
    o_ref[:, c0:] = _dot(xn_ref[HALO:, :], w_ref[:, c0:])


def in_proj_prompt(x, gain, w, cw, cb, seq, d_inner, tm=512, name="in_proj_p"):
    m, kdim = x.shape
    n = w.shape[1]
    conv_dim = cw.shape[1]
    tm = _row_tile(seq, tm)
    chunk = math.gcd(math.gcd(d_inner, conv_dim), IN_PROJ_CHUNK)
    assert tm % HALO == 0 and chunk % 128 == 0 and n > d_inner + conv_dim
    hb = tm // HALO
    return pl.pallas_call(
        functools.partial(_in_proj_kernel, tm, seq, d_inner, conv_dim, chunk),
        grid=(m // tm,),
        in_specs=[pl.BlockSpec((tm, kdim), lambda i: (i, 0)),
                  pl.BlockSpec((HALO, kdim), lambda i: (jnp.maximum(i * hb - 1, 0), 0)),
                  pl.BlockSpec((1, kdim), lambda i: (0, 0)),
                  pl.BlockSpec((kdim, n), lambda i: (0, 0), pipeline_mode=pl.Buffered(1)),
                  pl.BlockSpec(cw.shape, lambda i: (0, 0)),
                  pl.BlockSpec(cb.shape, lambda i: (0, 0))],
        out_specs=[pl.BlockSpec((tm, n), lambda i: (i, 0)),
                   pl.BlockSpec((1, 8, conv_dim), lambda i: (i, 0, 0))],
        out_shape=[jax.ShapeDtypeStruct((m, n), F32),
                   jax.ShapeDtypeStruct((m // tm, 8, conv_dim), F32)],
        scratch_shapes=[pltpu.VMEM((HALO + tm, kdim), BF16)],
        compiler_params=_cparams("parallel"),
        name=name,
    )(x, x, gain, w, cw, cb)


def _ssd_kernel(d_inner, n_heads, zx_ref, dtb_ref, alog_ref, dskip_ref, nw_ref,
                expand_ref, g_ref, state_ref, ht_ref, y_ref):
    q = SSD_CHUNK
    ns = SSM_D_STATE
    gn = SSM_GROUPS * ns
    conv_dim = d_inner + 2 * gn
    gw = d_inner // SSM_GROUPS
    c = pl.program_id(1)
    last = pl.num_programs(1) - 1

    @pl.when(c == 0)
    def _():
        ht_ref[...] = jnp.zeros(ht_ref.shape, F32)

    xs = zx_ref[:, d_inner:2 * d_inner]
    bm = zx_ref[:, 2 * d_inner:2 * d_inner + gn]
    cm = zx_ref[:, 2 * d_inner + gn:d_inner + conv_dim]

    dt_raw = zx_ref[:, d_inner + conv_dim:d_inner + conv_dim + 128] + dtb_ref[...]
    dt = jnp.maximum(dt_raw, 0.0) + jnp.log1p(jnp.exp(-jnp.abs(dt_raw)))
    da = dt * (-jnp.exp(alog_ref[...]))
    row = lax.broadcasted_iota(jnp.int32, (q, q), 0)
    col = lax.broadcasted_iota(jnp.int32, (q, q), 1)
    tri = row >= col
    a_cum = _dot_exact_lhs(jnp.where(tri, 1.0, 0.0).astype(BF16), da)
    a_cum_t = jnp.transpose(a_cum)

    expand = expand_ref[...]
    dt_hp = _dot_exact_rhs(dt, expand, parts=2)
    a_hp = _dot_exact_rhs(a_cum, expand)
    xdt = xs * dt_hp
    a_last = a_hp[q - 1:q, :]
    xdt_b = xdt.astype(BF16)
    dec_x = (jnp.exp(a_last - a_hp) * xdt).astype(BF16)
    off_scale = jnp.exp(a_hp)
    chunk_decay = jnp.exp(a_last)

    lane = lax.broadcasted_iota(jnp.int32, (q, 2 * SSM_HEAD_DIM), 1)
    heads_per_group = n_heads // SSM_GROUPS
    for g in range(SSM_GROUPS):
        bg = bm[:, g * ns:(g + 1) * ns]
        cg = cm[:, g * ns:(g + 1) * ns].astype(BF16)
        cb_mat = _dot_nt(cg, bg.astype(BF16))
        cols = slice(g * gw, (g + 1) * gw)
        ht = ht_ref[g]
        y_off = _dot(cg, ht.astype(BF16)) * off_scale[:, cols]
        new_states = _dot(jnp.transpose(bg).astype(BF16), dec_x[:, cols])
        ht_ref[g] = ht * chunk_decay[:, cols] + new_states
        for pr in range(heads_per_group // 2):
            c0 = g * gw + pr * 2 * SSM_HEAD_DIM
            x2 = xdt_b[:, c0:c0 + 2 * SSM_HEAD_DIM]
            mms, xhs = [], []
            for half in range(2):
                hh = g * heads_per_group + pr * 2 + half
                seg = a_cum[:, hh:hh + 1] - a_cum_t[hh:hh + 1, :]
                lmat = jnp.where(tri, jnp.exp(jnp.minimum(seg, 0.0)), 0.0)
                mms.append((cb_mat * lmat).astype(BF16))
                xhs.append(jnp.where((lane >= SSM_HEAD_DIM) == (half == 1), x2, jnp.zeros_like(x2)))
            yd = _dot(jnp.concatenate(mms, axis=1), jnp.concatenate(xhs, axis=0))
            y_ref[:, c0:c0 + 2 * SSM_HEAD_DIM] = yd + y_off[:, pr * 2 * SSM_HEAD_DIM:(pr + 1) * 2 * SSM_HEAD_DIM]

    y = y_ref[...] + dskip_ref[...] * xs
    gated = y * zx_ref[:, 0:d_inner]
    for g in range(SSM_GROUPS):
        cols = slice(g * gw, (g + 1) * gw)
        g_ref[:, cols] = (_rms_unit(gated[:, cols]) * nw_ref[:, cols]).astype(BF16)

    @pl.when(c == last)
    def _():
        for g in range(SSM_GROUPS):
            st = jnp.transpose(ht_ref[g])
            state_ref[0, g * heads_per_group:(g + 1) * heads_per_group] = st.reshape(
                heads_per_group, SSM_HEAD_DIM, ns)


def ssd_prompt(zx, batch, seq, d_inner, dtb, alog, dskip_hp, nw, expand, name="ssd"):
    m, width = zx.shape
    n_heads = d_inner // SSM_HEAD_DIM
    nc = seq // SSD_CHUNK
    q = SSD_CHUNK
    full = lambda shape: pl.BlockSpec(shape, lambda b, c: (0,) * len(shape))
    return pl.pallas_call(
        functools.partial(_ssd_kernel, d_inner, n_heads),
        grid=(batch, nc),
        in_specs=[pl.BlockSpec((q, width), lambda b, c: (b * nc + c, 0)),
                  full(dtb.shape), full(alog.shape),
                  full(dskip_hp.shape), full(nw.shape), full(expand.shape)],
        out_specs=[pl.BlockSpec((q, d_inner), lambda b, c: (b * nc + c, 0)),
                   pl.BlockSpec((1, n_heads, SSM_HEAD_DIM, SSM_D_STATE), lambda b, c: (b, 0, 0, 0))],
        out_shape=[jax.ShapeDtypeStruct((m, d_inner), BF16),
                   jax.ShapeDtypeStruct((batch, n_heads, SSM_HEAD_DIM, SSM_D_STATE), F32)],
        scratch_shapes=[pltpu.VMEM((SSM_GROUPS, SSM_D_STATE, d_inner // SSM_GROUPS), F32),
                        pltpu.VMEM((q, d_inner), F32)],
        compiler_params=_cparams("parallel", "arbitrary"),
        name=name,
    )(zx, dtb, alog, dskip_hp, nw, expand)


def _diff_lambda(lv, lambda_init):
    s1 = jnp.sum(lv[0:1, :] * lv[1:2, :], axis=-1, keepdims=True)
    s2 = jnp.sum(lv[2:3, :] * lv[3:4, :], axis=-1, keepdims=True)
    return jnp.exp(s1) - jnp.exp(s2) + lambda_init


ATTN_ROW_CHUNK = 64
ATTN_HEADS_PER_STEP = 2


def _attn_flat_kernel(tq, n_hps, nq, lambda_init, q_ref, k_ref, v_ref, lv_ref, sub_ref, o_ref, *scratch):
    hw = q_ref.shape[2] // n_hps
    hd = hw // 2
    rc = ATTN_ROW_CHUNK
    per_head = len(scratch) // n_hps
    heads = []
    for h in range(n_hps):
        r = scratch[h * per_head:(h + 1) * per_head]
        heads.append(dict(q2=r[0:2], f=r[2:4], s=r[4:6], p=r[6], m=r[7], c=r[8], acc=r[9],
                          cols=slice(h * hw, (h + 1) * hw)))
    ones = jnp.ones((tq, hw), BF16)

    def rows_of(tile):
        if isinstance(tile, int):
            return pl.ds(tile * tq, tq)
        return pl.ds(pl.multiple_of(tile * tq, tq), tq)

    def build_q2(hd_, tile, e):
        qh = q_ref[0, rows_of(tile), hd_["cols"]]
        lane = lax.broadcasted_iota(jnp.int32, qh.shape, 1)
        zero = jnp.zeros_like(qh)
        hd_["q2"][e][0:tq, :] = jnp.where(lane < hd, qh, zero)
        hd_["q2"][e][tq:, :] = jnp.where(lane >= hd, qh, zero)

    def init_tile(hd_):
        hd_["m"][...] = jnp.full(hd_["m"].shape, -jnp.inf, F32)
        hd_["acc"][...] = jnp.zeros(hd_["acc"].shape, F32)

    def qk(hd_, e, kb, dst):
        dst[...] = _dot_nt(hd_["q2"][e][...], k_ref[0, rows_of(kb), hd_["cols"]])

    def softmax_pv(hd_, kb, src, masked):
        m_ref, c_ref, p_ref, acc_ref = hd_["m"], hd_["c"], hd_["p"], hd_["acc"]
        for r0 in range(0, 2 * tq, rc):
            rows = slice(r0, r0 + rc)
            s = src[rows, :]
            if masked:
                r = lax.broadcasted_iota(jnp.int32, (rc, tq), 0) + (r0 % tq)
                cidx = lax.broadcasted_iota(jnp.int32, (rc, tq), 1)
                s = jnp.where(r >= cidx, s, -jnp.inf)
            m_old = m_ref[rows, :]
            m_new = jnp.maximum(m_old, jnp.max(s, axis=-1, keepdims=True))
            m_ref[rows, :] = m_new
            c_ref[rows, :] = jnp.exp(m_old - m_new)
            p_ref[rows, :] = jnp.exp(s - m_new).astype(BF16)
        v_ext = jnp.concatenate([v_ref[0, rows_of(kb), hd_["cols"]], ones], axis=1)
        pv = _dot(p_ref[...], v_ext)
        for r0 in range(0, 2 * tq, rc):
            rows = slice(r0, r0 + rc)
            acc_ref[rows, :] = acc_ref[rows, :] * c_ref[rows, :] + pv[rows, :]

    lam = _diff_lambda(lv_ref[...], lambda_init)

    def finalize(hd_, tile):
        acc = hd_["acc"][...]
        o = acc[:, :hw] / acc[:, hw:]
        o = o[:tq] - lam * o[tq:]
        o = _rms_unit(o) * sub_ref[...]
        o_ref[0, rows_of(tile), hd_["cols"]] = (o * (1.0 - lambda_init)).astype(BF16)

    def each(fn, *args):
        for hd_ in heads:
            fn(hd_, *args)

    def start_next(tile, e):
        nxt = jnp.minimum(tile + 1, nq - 1)
        each(build_q2, nxt, 1 - e)
        for hd_ in heads:
            qk(hd_, 1 - e, 0, hd_["f"][1 - e])

    def tile_body(tile, e, only_diagonal):
        each(init_tile)
        if only_diagonal:
            start_next(tile, e)
            for hd_ in heads:
                softmax_pv(hd_, 0, hd_["f"][e], True)
        else:
            for hd_ in heads:
                qk(hd_, e, 1, hd_["s"][0])
            for hd_ in heads:
                softmax_pv(hd_, 0, hd_["f"][e], False)

            def body(kk, carry):
                kb = 1 + 2 * kk
                for hd_ in heads:
                    qk(hd_, e, kb + 1, hd_["s"][1])
                for hd_ in heads:
                    softmax_pv(hd_, kb, hd_["s"][0], False)
                for hd_ in heads:
                    qk(hd_, e, kb + 2, hd_["s"][0])
                for hd_ in heads:
                    softmax_pv(hd_, kb + 1, hd_["s"][1], False)
                return carry

            lax.fori_loop(0, (tile - 1) // 2, body, 0)
            if e == 0:
                for hd_ in heads:
                    qk(hd_, e, tile, hd_["s"][1])
                for hd_ in heads:
                    softmax_pv(hd_, tile - 1, hd_["s"][0], False)
                start_next(tile, e)
                for hd_ in heads:
                    softmax_pv(hd_, tile, hd_["s"][1], True)
            else:
                start_next(tile, e)
                for hd_ in heads:
                    softmax_pv(hd_, tile, hd_["s"][0], True)
        each(finalize, tile)

    each(build_q2, 0, 0)
    for hd_ in heads:
        qk(hd_, 0, 0, hd_["f"][0])

    def pair_body(t, carry):
        @pl.when(t == 0)
        def _():
            tile_body(0, 0, True)

        @pl.when(t > 0)
        def _():
            tile_body(2 * t, 0, False)

        tile_body(2 * t + 1, 1, False)
        return carry

    lax.fori_loop(0, nq // 2, pair_body, 0)


def diff_attention_prompt_flat(q, k, v, lvec, subln, lambda_init, tq=512, name="diff_attn"):
    b, l, width = q.shape
    hw = width // ATTN_HEADS
    n_hps = ATTN_HEADS_PER_STEP if ATTN_HEADS % ATTN_HEADS_PER_STEP == 0 else 1
    bw = n_hps * hw
    tq = _row_tile(l, tq)
    nq = l // tq
    assert tq % ATTN_ROW_CHUNK == 0 and nq % 2 == 0
    head_scratch = ([pltpu.VMEM((2 * tq, hw), BF16)] * 2
                    + [pltpu.VMEM((2 * tq, tq), F32)] * 4
                    + [pltpu.VMEM((2 * tq, tq), BF16),
                       pltpu.VMEM((2 * tq, 1), F32),
                       pltpu.VMEM((2 * tq, 1), F32),
                       pltpu.VMEM((2 * tq, 2 * hw), F32)])
    seq_spec = pl.BlockSpec((1, l, bw), lambda bi, h: (bi, 0, h))
    return pl.pallas_call(
        functools.partial(_attn_flat_kernel, tq, n_hps, nq, lambda_init),
        grid=(b, ATTN_HEADS // n_hps),
        in_specs=[seq_spec, seq_spec, seq_spec,
                  pl.BlockSpec(lvec.shape, lambda bi, h: (0, 0)),
                  pl.BlockSpec(subln.shape, lambda bi, h: (0, 0))],
        out_specs=seq_spec,
        out_shape=jax.ShapeDtypeStruct((b, l, width), BF16),
        scratch_shapes=head_scratch * n_hps,
        compiler_params=_cparams("parallel", "parallel"),
        name=name,
    )(q, k, v, lvec, subln)


def _ssm_prep_kernel(d_inner, zx_ref, buf_ref, cw_ref, cb_ref, dtb_ref, alog_ref, expand_ref,
                     newbuf_ref, xs_ref, xdt_ref, b_ref, c_ref, decay_ref):
    gn = SSM_GROUPS * SSM_D_STATE
    conv_dim = d_inner + 2 * gn
    xnew = zx_ref[:, d_inner:d_inner + conv_dim]
    conv = cb_ref[...] + cw_ref[SSM_CONV_W - 1:SSM_CONV_W, :] * xnew
    for k in range(SSM_CONV_W - 1):
        conv = conv + cw_ref[k:k + 1, :] * buf_ref[k]
    for k in range(SSM_CONV_W - 2):
        newbuf_ref[k] = buf_ref[k + 1]
    newbuf_ref[SSM_CONV_W - 2] = xnew
    xbc = _silu(conv)
    xs = xbc[:, :d_inner]
    dt_raw = zx_ref[:, d_inner + conv_dim:d_inner + conv_dim + 128] + dtb_ref[...]
    dt = jnp.maximum(dt_raw, 0.0) + jnp.log1p(jnp.exp(-jnp.abs(dt_raw)))
    decay_ref[...] = jnp.exp(dt * (-jnp.exp(alog_ref[...])))
    xs_ref[...] = xs
    xdt_ref[...] = xs * _dot_exact_rhs(dt, expand_ref[...])
    b_ref[...] = xbc[:, d_inner:d_inner + gn]
    c_ref[...] = xbc[:, d_inner + gn:]


def _col_bcast(rowvec):
    n = rowvec.shape[1]
    wide = jnp.broadcast_to(rowvec, (128, n))
    return jnp.concatenate([jnp.transpose(wide[:, j * 128:(j + 1) * 128]) for j in range(n // 128)], axis=0)


def _ssm_state_kernel(n_heads, h0_ref, xdt_ref, decay_ref, b_ref, c_ref, hn_ref, y_ref):
    ns = SSM_D_STATE
    hpg = n_heads // SSM_GROUPS
    xcol = _col_bcast(xdt_ref[0])
    dcol = _col_bcast(decay_ref[0])
    for g in range(SSM_GROUPS):
        brow = b_ref[0, :, g * ns:(g + 1) * ns]
        crow = jnp.broadcast_to(c_ref[0, :, g * ns:(g + 1) * ns], (8, ns))
        chi, cmid, _ = _split3(crow)
        parts = []
        for r in range(hpg):
            h = g * hpg + r
            hn = (h0_ref[0, h] * dcol[h:h + 1, :]
                  + xcol[h * SSM_HEAD_DIM:(h + 1) * SSM_HEAD_DIM, :] * brow)
            hn_ref[0, h] = hn
            parts.append(hn)
        hhi, hmid, _ = _split3(jnp.concatenate(parts, axis=0))
        yr = _dot_nt(chi, hhi) + _dot_nt(cmid, hhi) + _dot_nt(chi, hmid)
        y_ref[0, :, g * hpg * SSM_HEAD_DIM:(g + 1) * hpg * SSM_HEAD_DIM] = yr[0:1, :]


def _ssm_post_kernel(d_inner, y_ref, xs_ref, zx_ref, dskip_ref, nw_ref, g_ref):
    gw = d_inner // SSM_GROUPS
    y = y_ref[...] + dskip_ref[...] * xs_ref[...]
    gated = y * _silu(zx_ref[:, 0:d_inner])
    for g in range(SSM_GROUPS):
        cols = slice(g * gw, (g + 1) * gw)
        g_ref[:, cols] = (_rms_unit(gated[:, cols]) * nw_ref[:, cols]).astype(BF16)


def ssm_step_sample(zx, conv_buf_t, h0, d_inner, cw, cb, dtb, alog, dskip_hp, nw, expand):
    bsz = zx.shape[0]
    n_heads = d_inner // SSM_HEAD_DIM
    gn = SSM_GROUPS * SSM_D_STATE
    conv_dim = d_inner + 2 * gn
    newbuf, xs, xdt, bm, cm, decay = pl.pallas_call(
        functools.partial(_ssm_prep_kernel, d_inner),
        out_shape=[jax.ShapeDtypeStruct(conv_buf_t.shape, F32),
                   jax.ShapeDtypeStruct((bsz, d_inner), F32),
                   jax.ShapeDtypeStruct((bsz, d_inner), F32),
                   jax.ShapeDtypeStruct((bsz, gn), F32),
                   jax.ShapeDtypeStruct((bsz, gn), F32),
                   jax.ShapeDtypeStruct((bsz, 128), F32)],
        compiler_params=pltpu.CompilerParams(vmem_limit_bytes=VMEM_LIMIT_BYTES),
        name="ssm_prep",
    )(zx, conv_buf_t, cw, cb, dtb, alog, expand)
    row3 = lambda a: a.reshape(bsz, 1, a.shape[1])
    hn, y = pl.pallas_call(
        functools.partial(_ssm_state_kernel, n_heads),
        grid=(bsz,),
        in_specs=[pl.BlockSpec((1, n_heads, SSM_HEAD_DIM, SSM_D_STATE), lambda b: (b, 0, 0, 0)),
                  pl.BlockSpec((1, 1, d_inner), lambda b: (b, 0, 0)),
                  pl.BlockSpec((1, 1, 128), lambda b: (b, 0, 0)),
                  pl.BlockSpec((1, 1, gn), lambda b: (b, 0, 0)),
                  pl.BlockSpec((1, 1, gn), lambda b: (b, 0, 0))],
        out_specs=[pl.BlockSpec((1, n_heads, SSM_HEAD_DIM, SSM_D_STATE), lambda b: (b, 0, 0, 0)),
                   pl.BlockSpec((1, 1, d_inner), lambda b: (b, 0, 0))],
        out_shape=[jax.ShapeDtypeStruct(h0.shape, F32),
                   jax.ShapeDtypeStruct((bsz, 1, d_inner), F32)],
        compiler_params=_cparams("parallel"),
        name="ssm_state",
    )(h0, row3(xdt), row3(decay), row3(bm), row3(cm))
    gact = pl.pallas_call(
        functools.partial(_ssm_post_kernel, d_inner),
        out_shape=jax.ShapeDtypeStruct((bsz, d_inner), BF16),
        compiler_params=pltpu.CompilerParams(vmem_limit_bytes=VMEM_LIMIT_BYTES),
        name="ssm_post",
    )(y.reshape(bsz, d_inner), xs, zx, dskip_hp, nw)
    return gact, newbuf, hn


def _ffn_gate_sample_kernel(f, gu_ref, buf_ref, cw_ref, cb_ref, h_ref, newbuf_ref):
    g = gu_ref[:, 0:f]
    u = gu_ref[:, f:2 * f]
    c = cb_ref[...] + cw_ref[FFN_CONV_W - 1:FFN_CONV_W, :] * g
    for k in range(FFN_CONV_W - 1):
        c = c + cw_ref[k:k + 1, :] * buf_ref[k]
    for k in range(FFN_CONV_W - 2):
        newbuf_ref[k] = buf_ref[k + 1]
    newbuf_ref[FFN_CONV_W - 2] = g
    h_ref[...] = (_silu(c) * u).astype(BF16)


def ffn_gate_sample(gu, buf_t, cw, cb):
    bsz = gu.shape[0]
    f = cw.shape[1]
    return pl.pallas_call(
        functools.partial(_ffn_gate_sample_kernel, f),
        out_shape=[jax.ShapeDtypeStruct((bsz, f), BF16),
                   jax.ShapeDtypeStruct(buf_t.shape, F32)],
        compiler_params=pltpu.CompilerParams(vmem_limit_bytes=VMEM_LIMIT_BYTES),
        name="ffn_gate_sample",
    )(gu, buf_t, cw, cb)


def _dec_attn_kernel(npg, lambda_init, pt_ref, q_ref, kn_ref, vn_ref, lv_ref, sub_ref, *rest):
    k_refs = rest[:npg]
    v_refs = rest[npg:2 * npg]
    o_ref = rest[2 * npg]
    s_ref, m_ref, l_ref, acc_ref = rest[2 * npg + 1:]
    del pt_ref
    j = pl.program_id(1)
    page, nh, hw = k_refs[0].shape[1:]
    hd = hw // 2
    sc, pc = DEC_SCORE_CHUNK, DEC_PROB_CHUNK

    qv = q_ref[0].astype(F32) * LOG2E
    srow = lax.broadcasted_iota(jnp.int32, (hw, hw), 0)
    scol = lax.broadcasted_iota(jnp.int32, (hw, hw), 1)
    seg = jnp.where(srow // hd == scol // hd, 1.0, 0.0).astype(BF16)

    def scores(k_tok):
        n_tok = k_tok.shape[0]
        prod = (k_tok * qv[None]).reshape(n_tok * nh, hw).astype(BF16)
        return _dot(prod, seg).reshape(n_tok, nh, hw)

    def swap_halves(x):
        return pltpu.roll(x.reshape(-1, hw), hd, 1).reshape(x.shape)

    @pl.when(j == 0)
    def _():
        m_ref[...] = scores(kn_ref[...])[0]
        l_ref[...] = jnp.ones(l_ref.shape, F32)
        vn = vn_ref[0]
        acc_ref[...] = jnp.concatenate([vn, vn], axis=1)

    gp = DEC_GROUP_PAGES
    n_groups = npg // gp

    def score_chunks(g):
        return [(p, t0) for p in range(g * gp, (g + 1) * gp) for t0 in range(0, page, sc)]

    def prob_chunks(g):
        return [(p, t0) for p in range(g * gp, (g + 1) * gp) for t0 in range(0, page, pc)]

    def score_chunk(p, t0, m_run):
        s = scores(k_refs[p][0, t0:t0 + sc])
        s_ref[p, t0:t0 + sc] = s
        return jnp.maximum(m_run, jnp.max(s, axis=0))

    m_cur = m_ref[...]
    l_run = l_ref[...]
    acc = acc_ref[...]
    acc_a, acc_b = acc[:, :hw], acc[:, hw:]
    m_next = m_cur
    for p, t0 in score_chunks(0):
        m_next = score_chunk(p, t0, m_next)
    for g in range(n_groups):
        m_new = m_next
        corr = jnp.exp2(m_cur - m_new)
        l_run = l_run * corr
        acc_a = acc_a * corr
        acc_b = acc_b * swap_halves(corr)
        m_cur = m_new
        pending = score_chunks(g + 1) if g + 1 < n_groups else []
        every = sc // pc
        for i, (p, t0) in enumerate(prob_chunks(g)):
            if pending and i % every == 0:
                m_next = score_chunk(*pending.pop(0), m_next)
            pr = jnp.exp2(s_ref[p, t0:t0 + pc] - m_new[None])
            l_run = l_run + jnp.sum(pr, axis=0)
            vp = v_refs[p][0, t0:t0 + pc]
            acc_a = acc_a + jnp.sum(pr * vp, axis=0)
            acc_b = acc_b + jnp.sum(swap_halves(pr) * vp, axis=0)
        assert not pending
    m_ref[...] = m_cur
    l_ref[...] = l_run
    acc_ref[...] = jnp.concatenate([acc_a, acc_b], axis=1)

    @pl.when(j == pl.num_programs(1) - 1)
    def _():
        xa = acc_a / l_run
        xb = acc_b / swap_halves(l_run)
        lam = _diff_lambda(lv_ref[...], lambda_init)
        low = lax.broadcasted_iota(jnp.int32, (nh, hw), 1) < hd
        o = jnp.where(low, xa - lam * xb, xb - lam * xa)
        o = _rms_unit(o) * sub_ref[...]
        o_ref[0] = (o * (1.0 - lambda_init)).astype(BF16)


def diff_attention_sample(q, k_new, v_new, cache_k, cache_v, page_table, lvec, subln, lambda_init):
    bsz, nh, hw = q.shape
    n_pages = page_table.shape[1]
    page = cache_k.shape[1]
    npg = min(PAGES_PER_STEP, n_pages)
    assert n_pages % npg == 0
    row = pl.BlockSpec((1, nh, hw), lambda b, j, pt: (b, 0, 0))

    def page_spec(p):
        return pl.BlockSpec((1, page, nh, hw), lambda b, j, pt: (pt[b, j * npg + p], 0, 0, 0))

    grid_spec = pltpu.PrefetchScalarGridSpec(
        num_scalar_prefetch=1,
        grid=(bsz, n_pages // npg),
        in_specs=[row, row, row,
                  pl.BlockSpec(lvec.shape, lambda b, j, pt: (0, 0)),
                  pl.BlockSpec(subln.shape, lambda b, j, pt: (0, 0))]
        + [page_spec(p) for p in range(npg)] + [page_spec(p) for p in range(npg)],
        out_specs=row,
        scratch_shapes=[pltpu.VMEM((npg, page, nh, hw), F32),
                        pltpu.VMEM((nh, hw), F32),
                        pltpu.VMEM((nh, hw), F32),
                        pltpu.VMEM((nh, 2 * hw), F32)],
    )
    return pl.pallas_call(
        functools.partial(_dec_attn_kernel, npg, lambda_init),
        grid_spec=grid_spec,
        out_shape=jax.ShapeDtypeStruct((bsz, nh, hw), BF16),
        compiler_params=_cparams("parallel", "arbitrary"),
        name="diff_attn_sample",
    )(page_table, q, k_new, v_new, lvec, subln, *([cache_k] * npg), *([cache_v] * npg))


def kernel(x_prompt, x_sample, state_ssm, state_conv_ssm, state_conv_ffn, cache_k, cache_v, page_table,
           norm_mix, norm_ffn, ssm_in_proj, ssm_conv_w, ssm_conv_b, ssm_dt_bias, ssm_a_log, ssm_d,
           ssm_norm, ssm_out_proj, ffn_up, ffn_conv_w, ffn_conv_b, ffn_down, norm_kv, w_k, w_v,
           w_q, lambda_q1, lambda_k1, lambda_q2, lambda_k2, subln, w_o, norm_final):
    bp, seq, d = x_prompt.shape
    bd = x_sample.shape[0]
    depth = norm_mix.shape[0]
    assert depth == 2 and ssm_in_proj.shape[0] == 1 and x_sample.shape[1] == 1
    d_inner = ssm_norm.shape[1]
    n_heads = ssm_dt_bias.shape[1]
    assert n_heads * SSM_HEAD_DIM == d_inner and n_heads <= 128
    conv_dim = ssm_conv_w.shape[2]
    f = ffn_conv_w.shape[2]
    attn_w = w_k.shape[1]
    hw = attn_w // ATTN_HEADS
    mp = bp * seq

    xp = x_prompt.reshape(mp, d)
    xs = x_sample.reshape(bd, d)

    in_dim = ssm_in_proj.shape[2]
    in_pad = -(-(d_inner + conv_dim + 128) // 768) * 768
    w_in = jnp.pad(ssm_in_proj[0].astype(BF16), ((0, 0), (0, in_pad - in_dim)))
    w_out = ssm_out_proj[0].astype(BF16)
    w_up_all, w_down_all = ffn_up.astype(BF16), ffn_down.astype(BF16)
    w_up = [w_up_all[i] for i in range(depth)]
    w_down = [w_down_all[i] for i in range(depth)]
    wk, wv, wq, wo = w_k.astype(BF16), w_v.astype(BF16), w_q[0].astype(BF16), w_o[0].astype(BF16)
    dtb = jnp.pad(ssm_dt_bias[0], (0, 128 - n_heads)).reshape(1, 128)
    alog = jnp.pad(ssm_a_log[0], (0, 128 - n_heads)).reshape(1, 128)
    dskip_hp = jnp.repeat(ssm_d[0], SSM_HEAD_DIM).reshape(1, d_inner)
    ssm_nw = ssm_norm[0].reshape(1, d_inner)
    expand = (lax.broadcasted_iota(jnp.int32, (128, d_inner), 0)
              == lax.broadcasted_iota(jnp.int32, (128, d_inner), 1) // SSM_HEAD_DIM).astype(BF16)
    cw0, cb0 = ssm_conv_w[0], ssm_conv_b[0].reshape(1, conv_dim)
    lambda_init = 0.8 - 0.6 * math.exp(-0.3 * 1)
    lvec = jnp.concatenate([lambda_q1, lambda_k1, lambda_q2, lambda_k2], axis=0)
    sub_w = subln[0].reshape(1, hw)
    scale = (hw // 2) ** -0.5

    g0 = norm_mix[0:1]
    tm_in = _row_tile(seq, 512)
    zx_p, tail_p = in_proj_prompt(xp, g0, w_in, cw0, cb0, seq, d_inner, tm=tm_in)
    gact_p, ssm_state_p = ssd_prompt(zx_p, bp, seq, d_inner, dtb, alog, dskip_hp, ssm_nw, expand)
    xp = matmul_residual(gact_p, w_out, xp, name="out_proj_p")
    ssm_conv_p = tail_p.reshape(bp, seq // tm_in, 8, conv_dim)[:, -1, 8 - (SSM_CONV_W - 1):, :]

    (zx_s,) = norm_matmul(xs, g0, [0], [w_in], [[F32]], tn=768, name="in_proj_s")
    gact_s, newbuf_s, ssm_state_s = ssm_step_sample(
        zx_s, jnp.swapaxes(state_conv_ssm[0], 0, 1), state_ssm[0], d_inner, cw0, cb0, dtb, alog,
        dskip_hp, ssm_nw, expand)
    xs = matmul_residual(gact_s, w_out, xs, name="out_proj_s")
    ssm_conv_s = jnp.swapaxes(newbuf_s, 0, 1)

    ffn_conv_p, ffn_conv_s = [], []

    def conv_ffn(i, xp, xs, final_gain):
        gain = norm_ffn[i:i + 1]
        cw, cb = ffn_conv_w[i], ffn_conv_b[i].reshape(1, f)
        tm = _row_tile(seq, 512)
        h_p, tail = ffn_up_prompt(xp, gain, w_up[i], cw, cb, seq, tm=tm, tn=f // 2, name=f"ffn_up_p{i}")
        xp = matmul_residual(h_p, w_down[i], xp, final_gain, name=f"ffn_down_p{i}")
        tiles_per_seq = seq // tm
        tail = tail.reshape(bp, tiles_per_seq, 8, f)[:, -1, 8 - (FFN_CONV_W - 1):, :]
        ffn_conv_p.append(tail)
        (gu,) = norm_matmul(xs, gain, [0], [w_up[i]], [[F32]], tn=f, name=f"ffn_up_s{i}")
        h_s, nb = ffn_gate_sample(gu, jnp.swapaxes(state_conv_ffn[i], 0, 1), cw, cb)
        xs = matmul_residual(h_s, w_down[i], xs, final_gain, name=f"ffn_down_s{i}")
        ffn_conv_s.append(jnp.swapaxes(nb, 0, 1))
        return xp, xs

    xp, xs = conv_ffn(0, xp, xs, None)

    gains = jnp.concatenate([norm_kv.reshape(1, d), norm_mix[1:2]], axis=0)
    kp, kp16, vp, vp16, qp16 = norm_matmul(
        xp, gains, [0, 0, 1], [wk, wv, wq], [[F32, BF16], [F32, BF16], [BF16]],
        scales=[1.0, 1.0, scale], tm=512, tn=attn_w, name="kvq_p")
    ks, vs, qs16 = norm_matmul(
        xs, gains, [0, 0, 1], [wk, wv, wq], [[F32], [F32], [BF16]],
        scales=[1.0, 1.0, scale], tn=attn_w, name="kvq_s")

    o_p = diff_attention_prompt_flat(qp16.reshape(bp, seq, attn_w), kp16.reshape(bp, seq, attn_w),
                                     vp16.reshape(bp, seq, attn_w), lvec, sub_w, lambda_init)
    xp = matmul_residual(o_p.reshape(mp, attn_w), wo, xp, name="attn_out_p")
    o_s = diff_attention_sample(qs16.reshape(bd, ATTN_HEADS, hw), ks.reshape(bd, ATTN_HEADS, hw),
                                vs.reshape(bd, ATTN_HEADS, hw), cache_k, cache_v, page_table,
                                lvec, sub_w, lambda_init)
    xs = matmul_residual(o_s.reshape(bd, attn_w), wo, xs, name="attn_out_s")

    xp, xs = conv_ffn(1, xp, xs, norm_final.reshape(1, d))

    return (xp.reshape(bp, seq, d), xs.reshape(bd, 1, d),
            ssm_state_p[None], ssm_state_s[None], ssm_conv_p[None], ssm_conv_s[None],
            jnp.stack(ffn_conv_p), jnp.stack(ffn_conv_s),
            kp.reshape(bp, seq, ATTN_HEADS, hw), vp.reshape(bp, seq, ATTN_HEADS, hw),
            ks.reshape(bd, 1, ATTN_HEADS, hw), vs.reshape(bd, 1, ATTN_HEADS, hw))
```

```python
import functools
import math

import jax
import jax.numpy as jnp
from jax import lax
from jax.experimental import pallas as pl
from jax.experimental.pallas import tpu as pltpu

F32 = jnp.float32
BF16 = jnp.bfloat16
EPS = 1e-5
LOG2E = 1.4426950408889634

SSM_HEAD_DIM = 64
SSM_GROUPS = 4
SSM_D_STATE = 128
SSM_CONV_W = 4
SSD_CHUNK = 128
ATTN_HEADS = 8
FFN_CONV_W = 3
PAGES_PER_STEP = 16
DEC_GROUP_PAGES = 2
DEC_SCORE_CHUNK = 16
DEC_PROB_CHUNK = 8

VMEM_LIMIT_BYTES = 56 * 1024 * 1024


def _cparams(*sem):
    return pltpu.CompilerParams(dimension_semantics=sem, vmem_limit_bytes=VMEM_LIMIT_BYTES)


def _silu(x):
    h = 0.5 * x
    return h + h * jnp.tanh(h)


def _rms_unit(x):
    return x * lax.rsqrt(jnp.mean(x * x, axis=-1, keepdims=True) + EPS)


def _split3(a):
    hi = a.astype(BF16)
    r = a - hi.astype(F32)
    mid = r.astype(BF16)
    lo = (r - mid.astype(F32)).astype(BF16)
    return hi, mid, lo


def _dot(a, b):
    return jnp.dot(a, b, preferred_element_type=F32)


def _dot_nt(a, b):
    return lax.dot_general(a, b, (((1,), (1,)), ((), ())), preferred_element_type=F32)


def _dot_exact_rhs(a, b_bf16, parts=3):
    hi, mid, lo = _split3(a)
    out = _dot(hi, b_bf16) + _dot(mid, b_bf16)
    return out + _dot(lo, b_bf16) if parts == 3 else out


def _dot_exact_lhs(a_bf16, b):
    hi, mid, lo = _split3(b)
    return _dot(a_bf16, hi) + _dot(a_bf16, mid) + _dot(a_bf16, lo)


def _row_tile(m, pref):
    t = min(m, pref)
    assert m % t == 0, (m, t)
    return t


def _norm_mm_kernel(n_w, gain_idx, scales, out_dtypes, x_ref, g_ref, *rest):
    w_refs = rest[:n_w]
    o_refs = rest[n_w:-1]
    xn_ref = rest[-1]

    @pl.when(pl.program_id(1) == 0)
    def _():
        xh = _rms_unit(x_ref[...])
        for i in range(n_w):
            gi = gain_idx[i]
            xn_ref[i] = (xh * g_ref[gi:gi + 1, :]).astype(BF16)

    k = 0
    for i in range(n_w):
        r = _dot(xn_ref[i], w_refs[i][...])
        if scales[i] != 1.0:
            r = r * scales[i]
        for dt in out_dtypes[i]:
            o_refs[k][...] = r.astype(dt)
            k += 1


def norm_matmul(x, gains, gain_idx, weights, out_dtypes, scales=None, tm=512, tn=None, name="norm_mm"):
    m, kdim = x.shape
    n = weights[0].shape[1]
    n_w = len(weights)
    scales = tuple(scales) if scales is not None else (1.0,) * n_w
    tm = _row_tile(m, tm)
    tn = n if tn is None else tn
    assert n % tn == 0
    outs, specs = [], []
    for dts in out_dtypes:
        for dt in dts:
            outs.append(jax.ShapeDtypeStruct((m, n), dt))
            specs.append(pl.BlockSpec((tm, tn), lambda i, j: (i, j)))
    kern = functools.partial(_norm_mm_kernel, n_w, tuple(gain_idx), scales,
                             tuple(tuple(d) for d in out_dtypes))
    return pl.pallas_call(
        kern,
        grid=(m // tm, n // tn),
        in_specs=[pl.BlockSpec((tm, kdim), lambda i, j: (i, 0)),
                  pl.BlockSpec(gains.shape, lambda i, j: (0, 0))]
        + [pl.BlockSpec((kdim, tn), lambda i, j: (0, j)) for _ in weights],
        out_specs=specs,
        out_shape=outs,
        scratch_shapes=[pltpu.VMEM((n_w, tm, kdim), BF16)],
        compiler_params=_cparams("parallel", "arbitrary"),
        name=name,
    )(x, gains, *weights)


def _mm_res_kernel(final_norm, a_ref, w_ref, r_ref, *rest):
    o_ref = rest[-1]
    y = r_ref[...] + _dot(a_ref[...], w_ref[...])
    if final_norm:
        y = _rms_unit(y) * rest[0][...]
    o_ref[...] = y


def matmul_residual(a, w, res, final_gain=None, tm=512, name="mm_res"):
    m, kdim = a.shape
    n = w.shape[1]
    tm = _row_tile(m, tm)
    ins = [a, w, res]
    specs = [pl.BlockSpec((tm, kdim), lambda i: (i, 0)),
             pl.BlockSpec((kdim, n), lambda i: (0, 0)),
             pl.BlockSpec((tm, n), lambda i: (i, 0))]
    if final_gain is not None:
        ins.append(final_gain)
        specs.append(pl.BlockSpec((1, n), lambda i: (0, 0)))
    return pl.pallas_call(
        functools.partial(_mm_res_kernel, final_gain is not None),
        grid=(m // tm,),
        in_specs=specs,
        out_specs=pl.BlockSpec((tm, n), lambda i: (i, 0)),
        out_shape=jax.ShapeDtypeStruct((m, n), F32),
        compiler_params=_cparams("parallel"),
        name=name,
    )(*ins)


HALO = 16
IN_PROJ_CHUNK = 256


def _ffn_up_kernel(tm, seq, f, tn, x_ref, halo_ref, g_ref, w_ref, cw_ref, cb_ref,
                   h_ref, tail_ref, xn_ref):
    i = pl.program_id(0)
    gain = g_ref[...]
    xn_ref[HALO:, :] = (_rms_unit(x_ref[...]) * gain).astype(BF16)
    keep = jnp.where((i * tm) % seq == 0, 0.0, 1.0)
    xn_ref[0:HALO, :] = (_rms_unit(halo_ref[...]) * gain * keep).astype(BF16)
    for c0 in range(0, f, tn):
        g = _dot(xn_ref[...], w_ref[:, c0:c0 + tn])
        u = _dot(xn_ref[HALO:, :], w_ref[:, f + c0:f + c0 + tn])
        c = (cw_ref[2:3, c0:c0 + tn] * g + cw_ref[1:2, c0:c0 + tn] * pltpu.roll(g, 1, 0)
             + cw_ref[0:1, c0:c0 + tn] * pltpu.roll(g, 2, 0) + cb_ref[:, c0:c0 + tn])
        c = c[HALO:, :]
        h_ref[:, c0:c0 + tn] = (_silu(c) * u).astype(BF16)
        tail_ref[0, :, c0:c0 + tn] = g[tm + HALO - 8:, :]


def ffn_up_prompt(x, gain, w_up, cw, cb, seq, tm=512, tn=None, name="ffn_up"):
    m, d = x.shape
    f = cw.shape[1]
    tm = _row_tile(seq, tm)
    tn = f if tn is None else tn
    assert f % tn == 0 and tm % HALO == 0
    hb = tm // HALO
    h, tail = pl.pallas_call(
        functools.partial(_ffn_up_kernel, tm, seq, f, tn),
        grid=(m // tm,),
        in_specs=[pl.BlockSpec((tm, d), lambda i: (i, 0)),
                  pl.BlockSpec((HALO, d), lambda i: (jnp.maximum(i * hb - 1, 0), 0)),
                  pl.BlockSpec((1, d), lambda i: (0, 0)),
                  pl.BlockSpec((d, 2 * f), lambda i: (0, 0), pipeline_mode=pl.Buffered(1)),
                  pl.BlockSpec((FFN_CONV_W, f), lambda i: (0, 0)),
                  pl.BlockSpec((1, f), lambda i: (0, 0))],
        out_specs=[pl.BlockSpec((tm, f), lambda i: (i, 0)),
                   pl.BlockSpec((1, 8, f), lambda i: (i, 0, 0))],
        out_shape=[jax.ShapeDtypeStruct((m, f), BF16),
                   jax.ShapeDtypeStruct((m // tm, 8, f), F32)],
        scratch_shapes=[pltpu.VMEM((HALO + tm, d), BF16)],
        compiler_params=_cparams("parallel"),
        name=name,
    )(x, x, gain, w_up, cw, cb)
    return h, tail


def _in_proj_kernel(tm, seq, d_inner, conv_dim, chunk, x_ref, halo_ref, g_ref, w_ref, cw_ref, cb_ref,
                    o_ref, tail_ref, xn_ref):
    i = pl.program_id(0)
    gain = g_ref[...]
    xn_ref[HALO:, :] = (_rms_unit(x_ref[...]) * gain).astype(BF16)
    keep = jnp.where((i * tm) % seq == 0, 0.0, 1.0)
    xn_ref[0:HALO, :] = (_rms_unit(halo_ref[...]) * gain * keep).astype(BF16)

    for c0 in range(0, d_inner, chunk):
        o_ref[:, c0:c0 + chunk] = _silu(_dot(xn_ref[HALO:, :], w_ref[:, c0:c0 + chunk]))
    for j0 in range(0, conv_dim, chunk):
        c0 = d_inner + j0
        g = _dot(xn_ref[...], w_ref[:, c0:c0 + chunk])
        conv = cb_ref[:, j0:j0 + chunk] + cw_ref[SSM_CONV_W - 1:SSM_CONV_W, j0:j0 + chunk] * g
        for k in range(1, SSM_CONV_W):
            conv = conv + cw_ref[SSM_CONV_W - 1 - k:SSM_CONV_W - k, j0:j0 + chunk] * pltpu.roll(g, k, 0)
        o_ref[:, c0:c0 + chunk] = _silu(conv[HALO:, :])
        tail_ref[0, :, j0:j0 + chunk] = g[tm + HALO - 8:, :]
    c0 = d_inner + conv_dim
    o_ref[:, c0:] = _dot(xn_ref[HALO:, :], w_ref[:, c0:])


def in_proj_prompt(x, gain, w, cw, cb, seq, d_inner, tm=512, name="in_proj_p"):
    m, kdim = x.shape
    n = w.shape[1]
    conv_dim = cw.shape[1]
    tm = _row_tile(seq, tm)
    chunk = math.gcd(math.gcd(d_inner, conv_dim), IN_PROJ_CHUNK)
    assert tm % HALO == 0 and chunk % 128 == 0 and n > d_inner + conv_dim
    hb = tm // HALO
    return pl.pallas_call(
        functools.partial(_in_proj_kernel, tm, seq, d_inner, conv_dim, chunk),
        grid=(m // tm,),
        in_specs=[pl.BlockSpec((tm, kdim), lambda i: (i, 0)),
                  pl.BlockSpec((HALO, kdim), lambda i: (jnp.maximum(i * hb - 1, 0), 0)),
                  pl.BlockSpec((1, kdim), lambda i: (0, 0)),
                  pl.BlockSpec((kdim, n), lambda i: (0, 0), pipeline_mode=pl.Buffered(1)),
                  pl.BlockSpec(cw.shape, lambda i: (0, 0)),
                  pl.BlockSpec(cb.shape, lambda i: (0, 0))],
        out_specs=[pl.BlockSpec((tm, n), lambda i: (i, 0)),
                   pl.BlockSpec((1, 8, conv_dim), lambda i: (i, 0, 0))],
        out_shape=[jax.ShapeDtypeStruct((m, n), F32),
                   jax.ShapeDtypeStruct((m // tm, 8, conv_dim), F32)],
        scratch_shapes=[pltpu.VMEM((HALO + tm, kdim), BF16)],
        compiler_params=_cparams("parallel"),
        name=name,
    )(x, x, gain, w, cw, cb)


def _ssd_kernel(d_inner, n_heads, zx_ref, dtb_ref, alog_ref, dskip_ref, nw_ref,
                expand_ref, g_ref, state_ref, ht_ref, y_ref):
    q = SSD_CHUNK
    ns = SSM_D_STATE
    gn = SSM_GROUPS * ns
    conv_dim = d_inner + 2 * gn
    gw = d_inner // SSM_GROUPS
    c = pl.program_id(1)
    last = pl.num_programs(1) - 1

    @pl.when(c == 0)
    def _():
        ht_ref[...] = jnp.zeros(ht_ref.shape, F32)

    xs = zx_ref[:, d_inner:2 * d_inner]
    bm = zx_ref[:, 2 * d_inner:2 * d_inner + gn]
    cm = zx_ref[:, 2 * d_inner + gn:d_inner + conv_dim]

    dt_raw = zx_ref[:, d_inner + conv_dim:d_inner + conv_dim + 128] + dtb_ref[...]
    dt = jnp.maximum(dt_raw, 0.0) + jnp.log1p(jnp.exp(-jnp.abs(dt_raw)))
    da = dt * (-jnp.exp(alog_ref[...]))
    row = lax.broadcasted_iota(jnp.int32, (q, q), 0)
    col = lax.broadcasted_iota(jnp.int32, (q, q), 1)
    tri = row >= col
    a_cum = _dot_exact_lhs(jnp.where(tri, 1.0, 0.0).astype(BF16), da)
    a_cum_t = jnp.transpose(a_cum)

    expand = expand_ref[...]
    dt_hp = _dot_exact_rhs(dt, expand, parts=2)
    a_hp = _dot_exact_rhs(a_cum, expand)
    xdt = xs * dt_hp
    a_last = a_hp[q - 1:q, :]
    xdt_b = xdt.astype(BF16)
    dec_x = (jnp.exp(a_last - a_hp) * xdt).astype(BF16)
    off_scale = jnp.exp(a_hp)
    chunk_decay = jnp.exp(a_last)

    lane = lax.broadcasted_iota(jnp.int32, (q, 2 * SSM_HEAD_DIM), 1)
    heads_per_group = n_heads // SSM_GROUPS
    for g in range(SSM_GROUPS):
        bg = bm[:, g * ns:(g + 1) * ns]
        cg = cm[:, g * ns:(g + 1) * ns].astype(BF16)
        cb_mat = _dot_nt(cg, bg.astype(BF16))
        cols = slice(g * gw, (g + 1) * gw)
        ht = ht_ref[g]
        y_off = _dot(cg, ht.astype(BF16)) * off_scale[:, cols]
        new_states = _dot(jnp.transpose(bg).astype(BF16), dec_x[:, cols])
        ht_ref[g] = ht * chunk_decay[:, cols] + new_states
        for pr in range(heads_per_group // 2):
            c0 = g * gw + pr * 2 * SSM_HEAD_DIM
            x2 = xdt_b[:, c0:c0 + 2 * SSM_HEAD_DIM]
            mms, xhs = [], []
            for half in range(2):
                hh = g * heads_per_group + pr * 2 + half
                seg = a_cum[:, hh:hh + 1] - a_cum_t[hh:hh + 1, :]
                lmat = jnp.where(tri, jnp.exp(jnp.minimum(seg, 0.0)), 0.0)
                mms.append((cb_mat * lmat).astype(BF16))
                xhs.append(jnp.where((lane >= SSM_HEAD_DIM) == (half == 1), x2, jnp.zeros_like(x2)))
            yd = _dot(jnp.concatenate(mms, axis=1), jnp.concatenate(xhs, axis=0))
            y_ref[:, c0:c0 + 2 * SSM_HEAD_DIM] = yd + y_off[:, pr * 2 * SSM_HEAD_DIM:(pr + 1) * 2 * SSM_HEAD_DIM]

    y = y_ref[...] + dskip_ref[...] * xs
    gated = y * zx_ref[:, 0:d_inner]
    for g in range(SSM_GROUPS):
        cols = slice(g * gw, (g + 1) * gw)
        g_ref[:, cols] = (_rms_unit(gated[:, cols]) * nw_ref[:, cols]).astype(BF16)

    @pl.when(c == last)
    def _():
        for g in range(SSM_GROUPS):
            st = jnp.transpose(ht_ref[g])
            state_ref[0, g * heads_per_group:(g + 1) * heads_per_group] = st.reshape(
                heads_per_group, SSM_HEAD_DIM, ns)


def ssd_prompt(zx, batch, seq, d_inner, dtb, alog, dskip_hp, nw, expand, name="ssd"):
    m, width = zx.shape
    n_heads = d_inner // SSM_HEAD_DIM
    nc = seq // SSD_CHUNK
    q = SSD_CHUNK
    full = lambda shape: pl.BlockSpec(shape, lambda b, c: (0,) * len(shape))
    return pl.pallas_call(
        functools.partial(_ssd_kernel, d_inner, n_heads),
        grid=(batch, nc),
        in_specs=[pl.BlockSpec((q, width), lambda b, c: (b * nc + c, 0)),
                  full(dtb.shape), full(alog.shape),
                  full(dskip_hp.shape), full(nw.shape), full(expand.shape)],
        out_specs=[pl.BlockSpec((q, d_inner), lambda b, c: (b * nc + c, 0)),
                   pl.BlockSpec((1, n_heads, SSM_HEAD_DIM, SSM_D_STATE), lambda b, c: (b, 0, 0, 0))],
        out_shape=[jax.ShapeDtypeStruct((m, d_inner), BF16),
                   jax.ShapeDtypeStruct((batch, n_heads, SSM_HEAD_DIM, SSM_D_STATE), F32)],
        scratch_shapes=[pltpu.VMEM((SSM_GROUPS, SSM_D_STATE, d_inner // SSM_GROUPS), F32),
                        pltpu.VMEM((q, d_inner), F32)],
        compiler_params=_cparams("parallel", "arbitrary"),
        name=name,
    )(zx, dtb, alog, dskip_hp, nw, expand)


def _diff_lambda(lv, lambda_init):
    s1 = jnp.sum(lv[0:1, :] * lv[1:2, :], axis=-1, keepdims=True)
    s2 = jnp.sum(lv[2:3, :] * lv[3:4, :], axis=-1, keepdims=True)
    return jnp.exp(s1) - jnp.exp(s2) + lambda_init


ATTN_ROW_CHUNK = 64
ATTN_HEADS_PER_STEP = 2


def _attn_flat_kernel(tq, n_hps, nq, lambda_init, q_ref, k_ref, v_ref, lv_ref, sub_ref, o_ref, *scratch):
    hw = q_ref.shape[2] // n_hps
    hd = hw // 2
    rc = ATTN_ROW_CHUNK
    per_head = len(scratch) // n_hps
    heads = []
    for h in range(n_hps):
        r = scratch[h * per_head:(h + 1) * per_head]
        heads.append(dict(q2=r[0:2], f=r[2:4], s=r[4:6], p=r[6], m=r[7], c=r[8], acc=r[9],
                          cols=slice(h * hw, (h + 1) * hw)))
    ones = jnp.ones((tq, hw), BF16)

    def rows_of(tile):
        if isinstance(tile, int):
            return pl.ds(tile * tq, tq)
        return pl.ds(pl.multiple_of(tile * tq, tq), tq)

    def build_q2(hd_, tile, e):
        qh = q_ref[0, rows_of(tile), hd_["cols"]]
        lane = lax.broadcasted_iota(jnp.int32, qh.shape, 1)
        zero = jnp.zeros_like(qh)
        hd_["q2"][e][0:tq, :] = jnp.where(lane < hd, qh, zero)
        hd_["q2"][e][tq:, :] = jnp.where(lane >= hd, qh, zero)

    def init_tile(hd_):
        hd_["m"][...] = jnp.full(hd_["m"].shape, -jnp.inf, F32)
        hd_["acc"][...] = jnp.zeros(hd_["acc"].shape, F32)

    def qk(hd_, e, kb, dst):
        dst[...] = _dot_nt(hd_["q2"][e][...], k_ref[0, rows_of(kb), hd_["cols"]])

    def softmax_pv(hd_, kb, src, masked):
        m_ref, c_ref, p_ref, acc_ref = hd_["m"], hd_["c"], hd_["p"], hd_["acc"]
        for r0 in range(0, 2 * tq, rc):
            rows = slice(r0, r0 + rc)
            s = src[rows, :]
            if masked:
                r = lax.broadcasted_iota(jnp.int32, (rc, tq), 0) + (r0 % tq)
                cidx = lax.broadcasted_iota(jnp.int32, (rc, tq), 1)
                s = jnp.where(r >= cidx, s, -jnp.inf)
            m_old = m_ref[rows, :]
            m_new = jnp.maximum(m_old, jnp.max(s, axis=-1, keepdims=True))
            m_ref[rows, :] = m_new
            c_ref[rows, :] = jnp.exp(m_old - m_new)
            p_ref[rows, :] = jnp.exp(s - m_new).astype(BF16)
        v_ext = jnp.concatenate([v_ref[0, rows_of(kb), hd_["cols"]], ones], axis=1)
        pv = _dot(p_ref[...], v_ext)
        for r0 in range(0, 2 * tq, rc):
            rows = slice(r0, r0 + rc)
            acc_ref[rows, :] = acc_ref[rows, :] * c_ref[rows, :] + pv[rows, :]

    lam = _diff_lambda(lv_ref[...], lambda_init)

    def finalize(hd_, tile):
        acc = hd_["acc"][...]
        o = acc[:, :hw] / acc[:, hw:]
        o = o[:tq] - lam * o[tq:]
        o = _rms_unit(o) * sub_ref[...]
        o_ref[0, rows_of(tile), hd_["cols"]] = (o * (1.0 - lambda_init)).astype(BF16)

    def each(fn, *args):
        for hd_ in heads:
            fn(hd_, *args)

    def start_next(tile, e):
        nxt = jnp.minimum(tile + 1, nq - 1)
        each(build_q2, nxt, 1 - e)
        for hd_ in heads:
            qk(hd_, 1 - e, 0, hd_["f"][1 - e])

    def tile_body(tile, e, only_diagonal):
        each(init_tile)
        if only_diagonal:
            start_next(tile, e)
            for hd_ in heads:
                softmax_pv(hd_, 0, hd_["f"][e], True)
        else:
            for hd_ in heads:
                qk(hd_, e, 1, hd_["s"][0])
            for hd_ in heads:
                softmax_pv(hd_, 0, hd_["f"][e], False)

            def body(kk, carry):
                kb = 1 + 2 * kk
                for hd_ in heads:
                    qk(hd_, e, kb + 1, hd_["s"][1])
                for hd_ in heads:
                    softmax_pv(hd_, kb, hd_["s"][0], False)
                for hd_ in heads:
                    qk(hd_, e, kb + 2, hd_["s"][0])
                for hd_ in heads:
                    softmax_pv(hd_, kb + 1, hd_["s"][1], False)
                return carry

            lax.fori_loop(0, (tile - 1) // 2, body, 0)
            if e == 0:
                for hd_ in heads:
                    qk(hd_, e, tile, hd_["s"][1])
                for hd_ in heads:
                    softmax_pv(hd_, tile - 1, hd_["s"][0], False)
                start_next(tile, e)
                for hd_ in heads:
                    softmax_pv(hd_, tile, hd_["s"][1], True)
            else:
                start_next(tile, e)
                for hd_ in heads:
                    softmax_pv(hd_, tile, hd_["s"][0], True)
        each(finalize, tile)

    each(build_q2, 0, 0)
    for hd_ in heads:
        qk(hd_, 0, 0, hd_["f"][0])

    def pair_body(t, carry):
        @pl.when(t == 0)
        def _():
            tile_body(0, 0, True)

        @pl.when(t > 0)
        def _():
            tile_body(2 * t, 0, False)

        tile_body(2 * t + 1, 1, False)
        return carry

    lax.fori_loop(0, nq // 2, pair_body, 0)


def diff_attention_prompt_flat(q, k, v, lvec, subln, lambda_init, tq=512, name="diff_attn"):
    b, l, width = q.shape
    hw = width // ATTN_HEADS
    n_hps = ATTN_HEADS_PER_STEP if ATTN_HEADS % ATTN_HEADS_PER_STEP == 0 else 1
    bw = n_hps * hw
    tq = _row_tile(l, tq)
    nq = l // tq
    assert tq % ATTN_ROW_CHUNK == 0 and nq % 2 == 0
    head_scratch = ([pltpu.VMEM((2 * tq, hw), BF16)] * 2
                    + [pltpu.VMEM((2 * tq, tq), F32)] * 4
                    + [pltpu.VMEM((2 * tq, tq), BF16),
                       pltpu.VMEM((2 * tq, 1), F32),
                       pltpu.VMEM((2 * tq, 1), F32),
                       pltpu.VMEM((2 * tq, 2 * hw), F32)])
    seq_spec = pl.BlockSpec((1, l, bw), lambda bi, h: (bi, 0, h))
    return pl.pallas_call(
        functools.partial(_attn_flat_kernel, tq, n_hps, nq, lambda_init),
        grid=(b, ATTN_HEADS // n_hps),
        in_specs=[seq_spec, seq_spec, seq_spec,
                  pl.BlockSpec(lvec.shape, lambda bi, h: (0, 0)),
                  pl.BlockSpec(subln.shape, lambda bi, h: (0, 0))],
        out_specs=seq_spec,
        out_shape=jax.ShapeDtypeStruct((b, l, width), BF16),
        scratch_shapes=head_scratch * n_hps,
        compiler_params=_cparams("parallel", "parallel"),
        name=name,
    )(q, k, v, lvec, subln)


def _ssm_prep_kernel(d_inner, zx_ref, buf_ref, cw_ref, cb_ref, dtb_ref, alog_ref, expand_ref,
                     newbuf_ref, xs_ref, xdt_ref, b_ref, c_ref, decay_ref):
    gn = SSM_GROUPS * SSM_D_STATE
    conv_dim = d_inner + 2 * gn
    xnew = zx_ref[:, d_inner:d_inner + conv_dim]
    conv = cb_ref[...] + cw_ref[SSM_CONV_W - 1:SSM_CONV_W, :] * xnew
    for k in range(SSM_CONV_W - 1):
        conv = conv + cw_ref[k:k + 1, :] * buf_ref[k]
    for k in range(SSM_CONV_W - 2):
        newbuf_ref[k] = buf_ref[k + 1]
    newbuf_ref[SSM_CONV_W - 2] = xnew
    xbc = _silu(conv)
    xs = xbc[:, :d_inner]
    dt_raw = zx_ref[:, d_inner + conv_dim:d_inner + conv_dim + 128] + dtb_ref[...]
    dt = jnp.maximum(dt_raw, 0.0) + jnp.log1p(jnp.exp(-jnp.abs(dt_raw)))
    decay_ref[...] = jnp.exp(dt * (-jnp.exp(alog_ref[...])))
    xs_ref[...] = xs
    xdt_ref[...] = xs * _dot_exact_rhs(dt, expand_ref[...])
    b_ref[...] = xbc[:, d_inner:d_inner + gn]
    c_ref[...] = xbc[:, d_inner + gn:]


def _col_bcast(rowvec):
    n = rowvec.shape[1]
    wide = jnp.broadcast_to(rowvec, (128, n))
    return jnp.concatenate([jnp.transpose(wide[:, j * 128:(j + 1) * 128]) for j in range(n // 128)], axis=0)


def _ssm_state_kernel(n_heads, h0_ref, xdt_ref, decay_ref, b_ref, c_ref, hn_ref, y_ref):
    ns = SSM_D_STATE
    hpg = n_heads // SSM_GROUPS
    xcol = _col_bcast(xdt_ref[0])
    dcol = _col_bcast(decay_ref[0])
    for g in range(SSM_GROUPS):
        brow = b_ref[0, :, g * ns:(g + 1) * ns]
        crow = jnp.broadcast_to(c_ref[0, :, g * ns:(g + 1) * ns], (8, ns))
        chi, cmid, _ = _split3(crow)
        parts = []
        for r in range(hpg):
            h = g * hpg + r
            hn = (h0_ref[0, h] * dcol[h:h + 1, :]
                  + xcol[h * SSM_HEAD_DIM:(h + 1) * SSM_HEAD_DIM, :] * brow)
            hn_ref[0, h] = hn
            parts.append(hn)
        hhi, hmid, _ = _split3(jnp.concatenate(parts, axis=0))
        yr = _dot_nt(chi, hhi) + _dot_nt(cmid, hhi) + _dot_nt(chi, hmid)
        y_ref[0, :, g * hpg * SSM_HEAD_DIM:(g + 1) * hpg * SSM_HEAD_DIM] = yr[0:1, :]


def _ssm_post_kernel(d_inner, y_ref, xs_ref, zx_ref, dskip_ref, nw_ref, g_ref):
    gw = d_inner // SSM_GROUPS
    y = y_ref[...] + dskip_ref[...] * xs_ref[...]
    gated = y * _silu(zx_ref[:, 0:d_inner])
    for g in range(SSM_GROUPS):
        cols = slice(g * gw, (g + 1) * gw)
        g_ref[:, cols] = (_rms_unit(gated[:, cols]) * nw_ref[:, cols]).astype(BF16)


def ssm_step_sample(zx, conv_buf_t, h0, d_inner, cw, cb, dtb, alog, dskip_hp, nw, expand):
    bsz = zx.shape[0]
    n_heads = d_inner // SSM_HEAD_DIM
    gn = SSM_GROUPS * SSM_D_STATE
    conv_dim = d_inner + 2 * gn
    newbuf, xs, xdt, bm, cm, decay = pl.pallas_call(
        functools.partial(_ssm_prep_kernel, d_inner),
        out_shape=[jax.ShapeDtypeStruct(conv_buf_t.shape, F32),
                   jax.ShapeDtypeStruct((bsz, d_inner), F32),
                   jax.ShapeDtypeStruct((bsz, d_inner), F32),
                   jax.ShapeDtypeStruct((bsz, gn), F32),
                   jax.ShapeDtypeStruct((bsz, gn), F32),
                   jax.ShapeDtypeStruct((bsz, 128), F32)],
        compiler_params=pltpu.CompilerParams(vmem_limit_bytes=VMEM_LIMIT_BYTES),
        name="ssm_prep",
    )(zx, conv_buf_t, cw, cb, dtb, alog, expand)
    row3 = lambda a: a.reshape(bsz, 1, a.shape[1])
    hn, y = pl.pallas_call(
        functools.partial(_ssm_state_kernel, n_heads),
        grid=(bsz,),
        in_specs=[pl.BlockSpec((1, n_heads, SSM_HEAD_DIM, SSM_D_STATE), lambda b: (b, 0, 0, 0)),
                  pl.BlockSpec((1, 1, d_inner), lambda b: (b, 0, 0)),
                  pl.BlockSpec((1, 1, 128), lambda b: (b, 0, 0)),
                  pl.BlockSpec((1, 1, gn), lambda b: (b, 0, 0)),
                  pl.BlockSpec((1, 1, gn), lambda b: (b, 0, 0))],
        out_specs=[pl.BlockSpec((1, n_heads, SSM_HEAD_DIM, SSM_D_STATE), lambda b: (b, 0, 0, 0)),
                   pl.BlockSpec((1, 1, d_inner), lambda b: (b, 0, 0))],
        out_shape=[jax.ShapeDtypeStruct(h0.shape, F32),
                   jax.ShapeDtypeStruct((bsz, 1, d_inner), F32)],
        compiler_params=_cparams("parallel"),
        name="ssm_state",
    )(h0, row3(xdt), row3(decay), row3(bm), row3(cm))
    gact = pl.pallas_call(
        functools.partial(_ssm_post_kernel, d_inner),
        out_shape=jax.ShapeDtypeStruct((bsz, d_inner), BF16),
        compiler_params=pltpu.CompilerParams(vmem_limit_bytes=VMEM_LIMIT_BYTES),
        name="ssm_post",
    )(y.reshape(bsz, d_inner), xs, zx, dskip_hp, nw)
    return gact, newbuf, hn


def _ffn_gate_sample_kernel(f, gu_ref, buf_ref, cw_ref, cb_ref, h_ref, newbuf_ref):
    g = gu_ref[:, 0:f]
    u = gu_ref[:, f:2 * f]
    c = cb_ref[...] + cw_ref[FFN_CONV_W - 1:FFN_CONV_W, :] * g
    for k in range(FFN_CONV_W - 1):
        c = c + cw_ref[k:k + 1, :] * buf_ref[k]
    for k in range(FFN_CONV_W - 2):
        newbuf_ref[k] = buf_ref[k + 1]
    newbuf_ref[FFN_CONV_W - 2] = g
    h_ref[...] = (_silu(c) * u).astype(BF16)


def ffn_gate_sample(gu, buf_t, cw, cb):
    bsz = gu.shape[0]
    f = cw.shape[1]
    return pl.pallas_call(
        functools.partial(_ffn_gate_sample_kernel, f),
        out_shape=[jax.ShapeDtypeStruct((bsz, f), BF16),
                   jax.ShapeDtypeStruct(buf_t.shape, F32)],
        compiler_params=pltpu.CompilerParams(vmem_limit_bytes=VMEM_LIMIT_BYTES),
        name="ffn_gate_sample",
    )(gu, buf_t, cw, cb)


def _dec_attn_kernel(npg, lambda_init, pt_ref, q_ref, kn_ref, vn_ref, lv_ref, sub_ref, *rest):
    k_refs = rest[:npg]
    v_refs = rest[npg:2 * npg]
    o_ref = rest[2 * npg]
    s_ref, m_ref, l_ref, acc_ref = rest[2 * npg + 1:]
    del pt_ref
    j = pl.program_id(1)
    page, nh, hw = k_refs[0].shape[1:]
    hd = hw // 2
    sc, pc = DEC_SCORE_CHUNK, DEC_PROB_CHUNK

    qv = q_ref[0].astype(F32) * LOG2E
    srow = lax.broadcasted_iota(jnp.int32, (hw, hw), 0)
    scol = lax.broadcasted_iota(jnp.int32, (hw, hw), 1)
    seg = jnp.where(srow // hd == scol // hd, 1.0, 0.0).astype(BF16)

    def scores(k_tok):
        n_tok = k_tok.shape[0]
        prod = (k_tok * qv[None]).reshape(n_tok * nh, hw).astype(BF16)
        return _dot(prod, seg).reshape(n_tok, nh, hw)

    def swap_halves(x):
        return pltpu.roll(x.reshape(-1, hw), hd, 1).reshape(x.shape)

    @pl.when(j == 0)
    def _():
        m_ref[...] = scores(kn_ref[...])[0]
        l_ref[...] = jnp.ones(l_ref.shape, F32)
        vn = vn_ref[0]
        acc_ref[...] = jnp.concatenate([vn, vn], axis=1)

    gp = DEC_GROUP_PAGES
    n_groups = npg // gp

    def score_chunks(g):
        return [(p, t0) for p in range(g * gp, (g + 1) * gp) for t0 in range(0, page, sc)]

    def prob_chunks(g):
        return [(p, t0) for p in range(g * gp, (g + 1) * gp) for t0 in range(0, page, pc)]

    def score_chunk(p, t0, m_run):
        s = scores(k_refs[p][0, t0:t0 + sc])
        s_ref[p, t0:t0 + sc] = s
        return jnp.maximum(m_run, jnp.max(s, axis=0))

    m_cur = m_ref[...]
    l_run = l_ref[...]
    acc = acc_ref[...]
    acc_a, acc_b = acc[:, :hw], acc[:, hw:]
    m_next = m_cur
    for p, t0 in score_chunks(0):
        m_next = score_chunk(p, t0, m_next)
    for g in range(n_groups):
        m_new = m_next
        corr = jnp.exp2(m_cur - m_new)
        l_run = l_run * corr
        acc_a = acc_a * corr
        acc_b = acc_b * swap_halves(corr)
        m_cur = m_new
        pending = score_chunks(g + 1) if g + 1 < n_groups else []
        every = sc // pc
        for i, (p, t0) in enumerate(prob_chunks(g)):
            if pending and i % every == 0:
                m_next = score_chunk(*pending.pop(0), m_next)
            pr = jnp.exp2(s_ref[p, t0:t0 + pc] - m_new[None])
            l_run = l_run + jnp.sum(pr, axis=0)
            vp = v_refs[p][0, t0:t0 + pc]
            acc_a = acc_a + jnp.sum(pr * vp, axis=0)
            acc_b = acc_b + jnp.sum(swap_halves(pr) * vp, axis=0)
        assert not pending
    m_ref[...] = m_cur
    l_ref[...] = l_run
    acc_ref[...] = jnp.concatenate([acc_a, acc_b], axis=1)

    @pl.when(j == pl.num_programs(1) - 1)
    def _():
        xa = acc_a / l_run
        xb = acc_b / swap_halves(l_run)
        lam = _diff_lambda(lv_ref[...], lambda_init)
        low = lax.broadcasted_iota(jnp.int32, (nh, hw), 1) < hd
        o = jnp.where(low, xa - lam * xb, xb - lam * xa)
        o = _rms_unit(o) * sub_ref[...]
        o_ref[0] = (o * (1.0 - lambda_init)).astype(BF16)


def diff_attention_sample(q, k_new, v_new, cache_k, cache_v, page_table, lvec, subln, lambda_init):
    bsz, nh, hw = q.shape
    n_pages = page_table.shape[1]
    page = cache_k.shape[1]
    npg = min(PAGES_PER_STEP, n_pages)
    assert n_pages % npg == 0
    row = pl.BlockSpec((1, nh, hw), lambda b, j, pt: (b, 0, 0))

    def page_spec(p):
        return pl.BlockSpec((1, page, nh, hw), lambda b, j, pt: (pt[b, j * npg + p], 0, 0, 0))

    grid_spec = pltpu.PrefetchScalarGridSpec(
        num_scalar_prefetch=1,
        grid=(bsz, n_pages // npg),
        in_specs=[row, row, row,
                  pl.BlockSpec(lvec.shape, lambda b, j, pt: (0, 0)),
                  pl.BlockSpec(subln.shape, lambda b, j, pt: (0, 0))]
        + [page_spec(p) for p in range(npg)] + [page_spec(p) for p in range(npg)],
        out_specs=row,
        scratch_shapes=[pltpu.VMEM((npg, page, nh, hw), F32),
                        pltpu.VMEM((nh, hw), F32),
                        pltpu.VMEM((nh, hw), F32),
                        pltpu.VMEM((nh, 2 * hw), F32)],
    )
    return pl.pallas_call(
        functools.partial(_dec_attn_kernel, npg, lambda_init),
        grid_spec=grid_spec,
        out_shape=jax.ShapeDtypeStruct((bsz, nh, hw), BF16),
        compiler_params=_cparams("parallel", "arbitrary"),
        name="diff_attn_sample",
    )(page_table, q, k_new, v_new, lvec, subln, *([cache_k] * npg), *([cache_v] * npg))


def kernel(x_prompt, x_sample, state_ssm, state_conv_ssm, state_conv_ffn, cache_k, cache_v, page_table,
           norm_mix, norm_ffn, ssm_in_proj, ssm_conv_w, ssm_conv_b, ssm_dt_bias, ssm_a_log, ssm_d,
           ssm_norm, ssm_out_proj, ffn_up, ffn_conv_w, ffn_conv_b, ffn_down, norm_kv, w_k, w_v,
           w_q, lambda_q1, lambda_k1, lambda_q2, lambda_k2, subln, w_o, norm_final):
    bp, seq, d = x_prompt.shape
    bd = x_sample.shape[0]
    depth = norm_mix.shape[0]
    assert depth == 2 and ssm_in_proj.shape[0] == 1 and x_sample.shape[1] == 1
    d_inner = ssm_norm.shape[1]
    n_heads = ssm_dt_bias.shape[1]
    assert n_heads * SSM_HEAD_DIM == d_inner and n_heads <= 128
    conv_dim = ssm_conv_w.shape[2]
    f = ffn_conv_w.shape[2]
    attn_w = w_k.shape[1]
    hw = attn_w // ATTN_HEADS
    mp = bp * seq

    xp = x_prompt.reshape(mp, d)
    xs = x_sample.reshape(bd, d)

    in_dim = ssm_in_proj.shape[2]
    in_pad = -(-(d_inner + conv_dim + 128) // 768) * 768
    w_in = jnp.pad(ssm_in_proj[0].astype(BF16), ((0, 0), (0, in_pad - in_dim)))
    w_out = ssm_out_proj[0].astype(BF16)
    w_up_all, w_down_all = ffn_up.astype(BF16), ffn_down.astype(BF16)
    w_up = [w_up_all[i] for i in range(depth)]
    w_down = [w_down_all[i] for i in range(depth)]
    wk, wv, wq, wo = w_k.astype(BF16), w_v.astype(BF16), w_q[0].astype(BF16), w_o[0].astype(BF16)
    dtb = jnp.pad(ssm_dt_bias[0], (0, 128 - n_heads)).reshape(1, 128)
    alog = jnp.pad(ssm_a_log[0], (0, 128 - n_heads)).reshape(1, 128)
    dskip_hp = jnp.repeat(ssm_d[0], SSM_HEAD_DIM).reshape(1, d_inner)
    ssm_nw = ssm_norm[0].reshape(1, d_inner)
    expand = (lax.broadcasted_iota(jnp.int32, (128, d_inner), 0)
              == lax.broadcasted_iota(jnp.int32, (128, d_inner), 1) // SSM_HEAD_DIM).astype(BF16)
    cw0, cb0 = ssm_conv_w[0], ssm_conv_b[0].reshape(1, conv_dim)
    lambda_init = 0.8 - 0.6 * math.exp(-0.3 * 1)
    lvec = jnp.concatenate([lambda_q1, lambda_k1, lambda_q2, lambda_k2], axis=0)
    sub_w = subln[0].reshape(1, hw)
    scale = (hw // 2) ** -0.5

    g0 = norm_mix[0:1]
    tm_in = _row_tile(seq, 512)
    zx_p, tail_p = in_proj_prompt(xp, g0, w_in, cw0, cb0, seq, d_inner, tm=tm_in)
    gact_p, ssm_state_p = ssd_prompt(zx_p, bp, seq, d_inner, dtb, alog, dskip_hp, ssm_nw, expand)
    xp = matmul_residual(gact_p, w_out, xp, name="out_proj_p")
    ssm_conv_p = tail_p.reshape(bp, seq // tm_in, 8, conv_dim)[:, -1, 8 - (SSM_CONV_W - 1):, :]

    (zx_s,) = norm_matmul(xs, g0, [0], [w_in], [[F32]], tn=768, name="in_proj_s")
    gact_s, newbuf_s, ssm_state_s = ssm_step_sample(
        zx_s, jnp.swapaxes(state_conv_ssm[0], 0, 1), state_ssm[0], d_inner, cw0, cb0, dtb, alog,
        dskip_hp, ssm_nw, expand)
    xs = matmul_residual(gact_s, w_out, xs, name="out_proj_s")
    ssm_conv_s = jnp.swapaxes(newbuf_s, 0, 1)

    ffn_conv_p, ffn_conv_s = [], []

    def conv_ffn(i, xp, xs, final_gain):
        gain = norm_ffn[i:i + 1]
        cw, cb = ffn_conv_w[i], ffn_conv_b[i].reshape(1, f)
        tm = _row_tile(seq, 512)
        h_p, tail = ffn_up_prompt(xp, gain, w_up[i], cw, cb, seq, tm=tm, tn=f // 2, name=f"ffn_up_p{i}")
        xp = matmul_residual(h_p, w_down[i], xp, final_gain, name=f"ffn_down_p{i}")
        tiles_per_seq = seq // tm
        tail = tail.reshape(bp, tiles_per_seq, 8, f)[:, -1, 8 - (FFN_CONV_W - 1):, :]
        ffn_conv_p.append(tail)
        (gu,) = norm_matmul(xs, gain, [0], [w_up[i]], [[F32]], tn=f, name=f"ffn_up_s{i}")
        h_s, nb = ffn_gate_sample(gu, jnp.swapaxes(state_conv_ffn[i], 0, 1), cw, cb)
        xs = matmul_residual(h_s, w_down[i], xs, final_gain, name=f"ffn_down_s{i}")
        ffn_conv_s.append(jnp.swapaxes(nb, 0, 1))
        return xp, xs

    xp, xs = conv_ffn(0, xp, xs, None)

    gains = jnp.concatenate([norm_kv.reshape(1, d), norm_mix[1:2]], axis=0)
    kp, kp16, vp, vp16, qp16 = norm_matmul(
        xp, gains, [0, 0, 1], [wk, wv, wq], [[F32, BF16], [F32, BF16], [BF16]],
        scales=[1.0, 1.0, scale], tm=512, tn=attn_w, name="kvq_p")
    ks, vs, qs16 = norm_matmul(
        xs, gains, [0, 0, 1], [wk, wv, wq], [[F32], [F32], [BF16]],
        scales=[1.0, 1.0, scale], tn=attn_w, name="kvq_s")

    o_p = diff_attention_prompt_flat(qp16.reshape(bp, seq, attn_w), kp16.reshape(bp, seq, attn_w),
                                     vp16.reshape(bp, seq, attn_w), lvec, sub_w, lambda_init)
    xp = matmul_residual(o_p.reshape(mp, attn_w), wo, xp, name="attn_out_p")
    o_s = diff_attention_sample(qs16.reshape(bd, ATTN_HEADS, hw), ks.reshape(bd, ATTN_HEADS, hw),
                                vs.reshape(bd, ATTN_HEADS, hw), cache_k, cache_v, page_table,
                                lvec, sub_w, lambda_init)
    xs = matmul_residual(o_s.reshape(bd, attn_w), wo, xs, name="attn_out_s")

    xp, xs = conv_ffn(1, xp, xs, norm_final.reshape(1, d))

    return (xp.reshape(bp, seq, d), xs.reshape(bd, 1, d),
            ssm_state_p[None], ssm_state_s[None], ssm_conv_p[None], ssm_conv_s[None],
            jnp.stack(ffn_conv_p), jnp.stack(ffn_conv_s),
            kp.reshape(bp, seq, ATTN_HEADS, hw), vp.reshape(bp, seq, ATTN_HEADS, hw),
            ks.reshape(bd, 1, ATTN_HEADS, hw), vs.reshape(bd, 1, ATTN_HEADS, hw))
```
